```python
import math
import jax, jax.numpy as jnp
from jax import lax
import numpy as np

D_MODEL = 4096
BATCH = 1
SEQ = 8192
DEPTH = 2

HEAD_DIM = 128
ROPE_THETA = 10000.0
MOBA_HEADS = D_MODEL // (2 * HEAD_DIM)
MOBA_BLOCK = 256
MOBA_TOPK = 3
MOBA_Q_CHUNK = 32
DIFF_HEADS = D_MODEL // (4 * HEAD_DIM)
DIFF_Q_BLOCK = 128
MOBA_W = MOBA_HEADS * HEAD_DIM
DIFF_W = DIFF_HEADS * 2 * HEAD_DIM
SPLIT0 = [MOBA_W, 2 * MOBA_W, 3 * MOBA_W, 3 * MOBA_W + DIFF_W, 3 * MOBA_W + 2 * DIFF_W]
IN0 = 3 * MOBA_W + 3 * DIFF_W
MLA_HEADS = D_MODEL // HEAD_DIM
Q_LORA = 1024
KV_LORA = 512
QK_NOPE = 128
QK_ROPE = 64
V_HEAD = 128
IDX_HEADS = 32
IDX_DIM = 128
IDX_ROPE = 64
IDX_TOPK_MAX = 256
DSA_Q_BLOCK = 128
SPLIT1 = [Q_LORA, Q_LORA + KV_LORA, Q_LORA + KV_LORA + QK_ROPE, Q_LORA + KV_LORA + QK_ROPE + IDX_DIM]
IN1 = Q_LORA + KV_LORA + QK_ROPE + IDX_DIM + IDX_HEADS
N_EXPERTS = 32
TOP_K = 4
D_EXPERT = 1024
SWIGLU_LIMIT = 7.0
SWIGLU_ALPHA = 1.702
DN_ALPHA = (2 * DEPTH) ** 0.25
DN_BETA = (8 * DEPTH) ** -0.25
LN_EPS = 1e-5
F32 = jnp.float32

kernel_name = "hybrid_moba_diff_dsa_moe_deepnorm"


def rope(x, pos):
    d = x.shape[-1]
    half = d // 2
    inv = ROPE_THETA ** (-jnp.arange(half, dtype=F32) / half)
    ang = pos.astype(F32)[:, None] * inv[None, :]
    ang = ang.reshape((1, ang.shape[0]) + (1,) * (x.ndim - 3) + (half,))
    cos, sin = jnp.cos(ang), jnp.sin(ang)
    xf = x.astype(F32)
    x1, x2 = xf[..., :half], xf[..., half:]
    return jnp.concatenate([x1 * cos - x2 * sin, x2 * cos + x1 * sin], axis=-1).astype(x.dtype)


def rmsnorm(x, g, eps=1e-6):
    xf = x.astype(F32)
    return (xf * lax.rsqrt(jnp.mean(xf * xf, -1, keepdims=True) + eps) * g.astype(F32)).astype(x.dtype)


def layernorm(x, g, b):
    xf = x.astype(F32)
    mu = jnp.mean(xf, -1, keepdims=True)
    var = jnp.mean(jnp.square(xf - mu), -1, keepdims=True)
    return ((xf - mu) * lax.rsqrt(var + LN_EPS) * g.astype(F32) + b.astype(F32)).astype(x.dtype)


def moba_attention(q, k, v):
    B, H, T, Dh = q.shape
    Tp = -(-T // MOBA_BLOCK) * MOBA_BLOCK
    pad = ((0, 0), (0, 0), (0, Tp - T), (0, 0))
    q, k, v = jnp.pad(q, pad), jnp.pad(k, pad), jnp.pad(v, pad)
    nb = Tp // MOBA_BLOCK
    kk = min(MOBA_TOPK, nb)
    scale = Dh ** -0.5
    kb = k.reshape(B, H, nb, MOBA_BLOCK, Dh)
    vb = v.reshape(B, H, nb, MOBA_BLOCK, Dh)
    kmean = jnp.mean(kb.astype(F32), axis=3)
    gate = jnp.einsum('bhtd,bhnd->bhtn', q.astype(F32), kmean)
    own = jnp.arange(Tp) // MOBA_BLOCK
    past = jnp.arange(nb)[None, :] < own[:, None]
    gate = jnp.where(past, gate, -jnp.inf)
    _, sel = lax.top_k(gate, kk)
    sel_ok = sel < own[None, None, :, None]
    nc = Tp // MOBA_Q_CHUNK
    qc = q.reshape(B, H, nc, MOBA_Q_CHUNK, Dh).transpose(2, 0, 1, 3, 4)
    selc = sel.reshape(B, H, nc, MOBA_Q_CHUNK, kk).transpose(2, 0, 1, 3, 4)
    okc = sel_ok.reshape(B, H, nc, MOBA_Q_CHUNK, kk).transpose(2, 0, 1, 3, 4)
    b_ix = jnp.arange(B)[:, None, None, None]
    h_ix = jnp.arange(H)[None, :, None, None]

    def chunk(args):
        c, qi, si, oki = args
        t = c * MOBA_Q_CHUNK + jnp.arange(MOBA_Q_CHUNK)
        blk = (c * MOBA_Q_CHUNK) // MOBA_BLOCK
        kg = kb[b_ix, h_ix, si]
        vg = vb[b_ix, h_ix, si].reshape(B, H, MOBA_Q_CHUNK, kk * MOBA_BLOCK, Dh)
        s_sel = jnp.einsum('bhqd,bhqjpd->bhqjp', qi, kg).astype(F32) * scale
        s_sel = jnp.where(oki[..., None], s_sel, -jnp.inf).reshape(B, H, MOBA_Q_CHUNK, kk * MOBA_BLOCK)
        k_own = lax.dynamic_slice_in_dim(kb, blk, 1, axis=2)[:, :, 0]
        v_own = lax.dynamic_slice_in_dim(vb, blk, 1, axis=2)[:, :, 0]
        s_own = jnp.einsum('bhqd,bhpd->bhqp', qi, k_own).astype(F32) * scale
        kpos = blk * MOBA_BLOCK + jnp.arange(MOBA_BLOCK)
        s_own = jnp.where(kpos[None, :] <= t[:, None], s_own, -jnp.inf)
        p = jax.nn.softmax(jnp.concatenate([s_sel, s_own], axis=-1), axis=-1).astype(v.dtype)
        n_sel = kk * MOBA_BLOCK
        return (jnp.einsum('bhqn,bhqnd->bhqd', p[..., :n_sel], vg)
                + jnp.einsum('bhqp,bhpd->bhqd', p[..., n_sel:], v_own))

    out = lax.map(chunk, (jnp.arange(nc), qc, selc, okc))
    return out.transpose(1, 2, 0, 3, 4).reshape(B, H, Tp, Dh)[:, :, :T]


def diff_attention(q, k, v, lam):
    B, H, _, T, Dh = q.shape
    nq = T // DIFF_Q_BLOCK
    scale = Dh ** -0.5
    qb = q.reshape(B, H, 2, nq, DIFF_Q_BLOCK, Dh).transpose(3, 0, 1, 2, 4, 5)
    kpos = jnp.arange(T)

    def block(args):
        i, qi = args
        s = jnp.einsum('bhmqd,bhmkd->bhmqk', qi, k).astype(F32) * scale
        qpos = i * DIFF_Q_BLOCK + jnp.arange(DIFF_Q_BLOCK)
        s = jnp.where(kpos[None, :] <= qpos[:, None], s, -jnp.inf)
        p = jax.nn.softmax(s, axis=-1)
        a = (p[:, :, 0] - lam * p[:, :, 1]).astype(v.dtype)
        return jnp.einsum('bhqk,bhke->bhqe', a, v)

    o = lax.map(block, (jnp.arange(nq), qb))
    return o.transpose(1, 0, 3, 2, 4).reshape(B, T, H, 2 * Dh)


def dsa_attention(q_lat, q_pe, ckv, k_pe, q_idx, k_idx, w_idx):
    B, T, H, C = q_lat.shape
    n_sel = min(IDX_TOPK_MAX, T // 4)
    nq = T // DSA_Q_BLOCK
    scale = (QK_NOPE + QK_ROPE) ** -0.5
    kpos = jnp.arange(T)
    b_ix = jnp.arange(B)[:, None, None]

    def split_blocks(a):
        return jnp.moveaxis(a.reshape((B, nq, DSA_Q_BLOCK) + a.shape[2:]), 1, 0)

    def block(args):
        i, ql, qp, qi, wi = args
        qpos = i * DSA_Q_BLOCK + jnp.arange(DSA_Q_BLOCK)
        isc = jax.nn.relu(jnp.einsum('bqhd,bsd->bqhs', qi, k_idx).astype(F32))
        isc = jnp.einsum('bqhs,bqh->bqs', isc, wi.astype(F32))
        isc = jnp.where(kpos[None, None, :] <= qpos[None, :, None], isc, -jnp.inf)
        _, sel = lax.top_k(isc, n_sel)
        ok = sel <= qpos[None, :, None]
        cg = ckv[b_ix, sel]
        pg = k_pe[b_ix, sel]
        s = (jnp.einsum('bqhc,bqkc->bhqk', ql, cg)
             + jnp.einsum('bqhr,bqkr->bhqk', qp, pg)).astype(F32) * scale
        s = jnp.where(ok[:, None], s, -jnp.inf)
        p = jax.nn.softmax(s, axis=-1).astype(ckv.dtype)
        return jnp.einsum('bhqk,bqkc->bqhc', p, cg)

    o = lax.map(block, (jnp.arange(nq), split_blocks(q_lat), split_blocks(q_pe),
                        split_blocks(q_idx), split_blocks(w_idx)))
    return jnp.moveaxis(o, 0, 1).reshape(B, T, H, C)


def even_mixer(x, w_in, w_out, lam_q1, lam_k1, lam_q2, lam_k2, g_subln, layer_idx):
    B, T, _ = x.shape
    pos = jnp.arange(T)
    h = x @ w_in
    mq, mk, mv, dq, dk, dv = jnp.split(h, SPLIT0, axis=-1)
    mq = rope(mq.reshape(B, T, MOBA_HEADS, HEAD_DIM), pos).transpose(0, 2, 1, 3)
    mk = rope(mk.reshape(B, T, MOBA_HEADS, HEAD_DIM), pos).transpose(0, 2, 1, 3)
    mv = mv.reshape(B, T, MOBA_HEADS, HEAD_DIM).transpose(0, 2, 1, 3)
    o_moba = moba_attention(mq, mk, mv).transpose(0, 2, 1, 3).reshape(B, T, MOBA_W)
    dq = rope(dq.reshape(B, T, DIFF_HEADS, 2, HEAD_DIM), pos).transpose(0, 2, 3, 1, 4)
    dk = rope(dk.reshape(B, T, DIFF_HEADS, 2, HEAD_DIM), pos).transpose(0, 2, 3, 1, 4)
    dv = dv.reshape(B, T, DIFF_HEADS, 2 * HEAD_DIM).transpose(0, 2, 1, 3)
    lam_init = 0.8 - 0.6 * math.exp(-0.3 * layer_idx)
    lam = (jnp.exp(jnp.sum(lam_q1.astype(F32) * lam_k1.astype(F32)))
           - jnp.exp(jnp.sum(lam_q2.astype(F32) * lam_k2.astype(F32))) + lam_init)
    o_diff = diff_attention(dq, dk, dv, lam)
    o_diff = (rmsnorm(o_diff, g_subln, eps=1e-5) * (1.0 - lam_init)).reshape(B, T, DIFF_W)
    return jnp.concatenate([o_moba, o_diff], axis=-1) @ w_out


def odd_mixer(x, w_in, g_q, g_kv, w_qb, w_uk, w_uv, w_iq, g_kidx, b_kidx, w_out):
    B, T, _ = x.shape
    pos = jnp.arange(T)
    h = x @ w_in
    cq, ckv, kpe, kidx, widx = jnp.split(h, SPLIT1, axis=-1)
    cq = rmsnorm(cq, g_q)
    ckv = rmsnorm(ckv, g_kv)
    q = (cq @ w_qb).reshape(B, T, MLA_HEADS, QK_NOPE + QK_ROPE)
    q_nope = q[..., :QK_NOPE]
    q_pe = rope(q[..., QK_NOPE:], pos)
    k_pe = rope(kpe, pos)
    q_lat = jnp.einsum('bthd,chd->bthc', q_nope, w_uk.reshape(KV_LORA, MLA_HEADS, QK_NOPE))
    qi = (cq @ w_iq).reshape(B, T, IDX_HEADS, IDX_DIM)
    qi = jnp.concatenate([rope(qi[..., :IDX_ROPE], pos), qi[..., IDX_ROPE:]], axis=-1)
    ki = layernorm(kidx, g_kidx, b_kidx)
    ki = jnp.concatenate([rope(ki[..., :IDX_ROPE], pos), ki[..., IDX_ROPE:]], axis=-1)
    wi = widx * (IDX_HEADS ** -0.5 * IDX_DIM ** -0.5)
    o_lat = dsa_attention(q_lat, q_pe, ckv, k_pe, qi, ki, wi)
    o = jnp.einsum('bthc,chd->bthd', o_lat, w_uv.reshape(KV_LORA, MLA_HEADS, V_HEAD))
    return o.reshape(B, T, MLA_HEADS * V_HEAD) @ w_out


def moe(x, w_router, b_router, w_gu, b_gu, w_down, b_down):
    B, T, D = x.shape
    xt = x.reshape(B * T, D)
    logits = (xt @ w_router + b_router).astype(F32)
    top_v, top_i = lax.top_k(logits, TOP_K)
    gates = jax.nn.softmax(top_v, axis=-1)
    comb = jnp.sum(jnp.where(top_i[..., None] == jnp.arange(N_EXPERTS), gates[..., None], 0.0), axis=1)
    out = jnp.zeros((B * T, D), F32)
    for e in range(N_EXPERTS):
        hg = xt @ w_gu[e] + b_gu[e]
        g = jnp.minimum(hg[:, ::2], SWIGLU_LIMIT)
        u = jnp.clip(hg[:, 1::2], -SWIGLU_LIMIT, SWIGLU_LIMIT)
        y = ((u + 1.0) * (g * jax.nn.sigmoid(g * SWIGLU_ALPHA))) @ w_down[e] + b_down[e]
        out = out + comb[:, e:e + 1] * y.astype(F32)
    return out.astype(x.dtype).reshape(B, T, D)


def setup_inputs(seed: int = 0) -> dict:
    key = jax.random.key(seed)
    ks = jax.random.split(key, 32)
    nrm = lambda k, shape, s: jax.random.normal(k, shape, F32) * s
    D = D_MODEL
    return {
        "x": nrm(ks[0], (BATCH, SEQ, D), 1.0),
        "l0_w_in": nrm(ks[1], (D, IN0), D ** -0.5),
        "l0_w_out": nrm(ks[2], (MOBA_W + DIFF_W, D), (MOBA_W + DIFF_W) ** -0.5 * DN_BETA),
        "l0_lam_q1": nrm(ks[3], (HEAD_DIM,), 0.1),
        "l0_lam_k1": nrm(ks[4], (HEAD_DIM,), 0.1),
        "l0_lam_q2": nrm(ks[5], (HEAD_DIM,), 0.1),
        "l0_lam_k2": nrm(ks[6], (HEAD_DIM,), 0.1),
        "l0_g_subln": 1.0 + nrm(ks[7], (2 * HEAD_DIM,), 0.01),
        "l1_w_in": nrm(ks[8], (D, IN1), D ** -0.5),
        "l1_g_q": 1.0 + nrm(ks[9], (Q_LORA,), 0.01),
        "l1_g_kv": 1.0 + nrm(ks[10], (KV_LORA,), 0.01),
        "l1_w_qb": nrm(ks[11], (Q_LORA, MLA_HEADS * (QK_NOPE + QK_ROPE)), Q_LORA ** -0.5),
        "l1_w_uk": nrm(ks[12], (KV_LORA, MLA_HEADS * QK_NOPE), KV_LORA ** -0.5),
        "l1_w_uv": nrm(ks[13], (KV_LORA, MLA_HEADS * V_HEAD), KV_LORA ** -0.5),
        "l1_w_iq": nrm(ks[14], (Q_LORA, IDX_HEADS * IDX_DIM), Q_LORA ** -0.5),
        "l1_g_kidx": 1.0 + nrm(ks[15], (IDX_DIM,), 0.01),
        "l1_b_kidx": nrm(ks[16], (IDX_DIM,), 0.01),
        "l1_w_out": nrm(ks[17], (MLA_HEADS * V_HEAD, D), (MLA_HEADS * V_HEAD) ** -0.5 * DN_BETA),
        "ln_g": 1.0 + nrm(ks[18], (DEPTH, 2, D), 0.01),
        "ln_b": nrm(ks[19], (DEPTH, 2, D), 0.01),
        "moe_w_router": nrm(ks[20], (DEPTH, D, N_EXPERTS), D ** -0.5),
        "moe_b_router": nrm(ks[21], (DEPTH, N_EXPERTS), 0.01),
        "moe_w_gu": nrm(ks[22], (DEPTH, N_EXPERTS, D, 2 * D_EXPERT), D ** -0.5),
        "moe_b_gu": nrm(ks[23], (DEPTH, N_EXPERTS, 2 * D_EXPERT), 0.01),
        "moe_w_down": nrm(ks[24], (DEPTH, N_EXPERTS, D_EXPERT, D), D_EXPERT ** -0.5 * DN_BETA),
        "moe_b_down": nrm(ks[25], (DEPTH, N_EXPERTS, D), 0.01),
    }


def reference(x, l0_w_in, l0_w_out, l0_lam_q1, l0_lam_k1, l0_lam_q2, l0_lam_k2, l0_g_subln,
              l1_w_in, l1_g_q, l1_g_kv, l1_w_qb, l1_w_uk, l1_w_uv, l1_w_iq, l1_g_kidx, l1_b_kidx,
              l1_w_out, ln_g, ln_b, moe_w_router, moe_b_router, moe_w_gu, moe_b_gu,
              moe_w_down, moe_b_down):
    mixer_params = [
        (l0_w_in, l0_w_out, l0_lam_q1, l0_lam_k1, l0_lam_q2, l0_lam_k2, l0_g_subln),
        (l1_w_in, l1_g_q, l1_g_kv, l1_w_qb, l1_w_uk, l1_w_uv, l1_w_iq, l1_g_kidx, l1_b_kidx, l1_w_out),
    ]
    for i in range(DEPTH):
        if i % 2 == 0:
            mix = even_mixer(x, *mixer_params[i], i)
        else:
            mix = odd_mixer(x, *mixer_params[i])
        x = layernorm(DN_ALPHA * x + mix, ln_g[i, 0], ln_b[i, 0])
        f = moe(x, moe_w_router[i], moe_b_router[i], moe_w_gu[i], moe_b_gu[i],
                moe_w_down[i], moe_b_down[i])
        x = layernorm(DN_ALPHA * x + f, ln_g[i, 1], ln_b[i, 1])
    return x
```

```python
import functools
import math

import jax
import jax.numpy as jnp
from jax import lax
from jax.experimental import pallas as pl
from jax.experimental.pallas import tpu as pltpu

F32 = jnp.float32
BF16 = jnp.bfloat16
I32 = jnp.int32

HEAD_DIM = 128
ROPE_THETA = 10000.0
MOBA_HEADS = 16
MOBA_BLOCK = 256
MOBA_TOPK = 3
DIFF_HEADS = 8
MOBA_W = MOBA_HEADS * HEAD_DIM
DIFF_W = DIFF_HEADS * 2 * HEAD_DIM
MLA_HEADS = 32
Q_LORA = 1024
KV_LORA = 512
QK_NOPE = 128
QK_ROPE = 64
V_HEAD = 128
IDX_HEADS = 32
IDX_DIM = 128
IDX_TOPK_MAX = 256
N_EXPERTS = 32
TOP_K = 4
SWIGLU_LIMIT = 7.0
SWIGLU_ALPHA = 1.702
DEPTH = 2
DN_ALPHA = (2 * DEPTH) ** 0.25
LN_EPS = 1e-5

LANES = 128
VMEM_LIMIT_BYTES = 56 * 2**20

NEG = -1e30
KEY_NEG_INF = -2139095041


def _cparams(semantics):
    return pltpu.CompilerParams(dimension_semantics=semantics, vmem_limit_bytes=VMEM_LIMIT_BYTES)


def _dot(a, b):
    return jnp.dot(a, b, preferred_element_type=F32)


def _dot_nt(a, b, precision=None):
    return lax.dot_general(a, b, (((1,), (1,)), ((), ())), preferred_element_type=F32, precision=precision)


def _rope_tables(T):
    pos = jnp.arange(T).astype(F32)[:, None]
    inv64 = ROPE_THETA ** (-jnp.arange(64, dtype=F32) / 64)
    a64 = pos * inv64[None, :]
    c64, s64 = jnp.cos(a64), jnp.sin(a64)
    inv32 = ROPE_THETA ** (-jnp.arange(32, dtype=F32) / 32)
    a32 = pos * inv32[None, :]
    c32, s32 = jnp.cos(a32), jnp.sin(a32)
    one, zero = jnp.ones_like(c32), jnp.zeros_like(c32)
    full = (jnp.concatenate([c64, c64], 1), jnp.concatenate([-s64, s64], 1))
    pair = (jnp.concatenate([c32, c32, c32, c32], 1), jnp.concatenate([-s32, s32, -s32, s32], 1))
    idx = (jnp.concatenate([c32, c32, one, one], 1), jnp.concatenate([-s32, s32, zero, zero], 1))
    return full, pair, idx


def _swap_full(z):
    return pltpu.roll(z, 64, axis=1)


def _swap_pair(z):
    lane = lax.broadcasted_iota(I32, z.shape, 1)
    return jnp.where((lane % 64) < 32, pltpu.roll(z, 96, axis=1), pltpu.roll(z, 32, axis=1))


def _rope128(z, c, s, swap):
    return z * c + swap(z) * s


def _mm_kernel(*refs, nk, tn, rope_ranges, swaps):
    n_tab = len(swaps)
    a_ref, b_ref = refs[0], refs[1]
    tab_refs = refs[2:2 + 2 * n_tab]
    o_ref, acc_ref = refs[2 + 2 * n_tab], refs[3 + 2 * n_tab]
    k = pl.program_id(2)

    @pl.when(k == 0)
    def _():
        acc_ref[...] = jnp.zeros_like(acc_ref)

    acc_ref[...] += _dot(a_ref[...].astype(BF16), b_ref[...].astype(BF16))

    @pl.when(k == nk - 1)
    def _():
        if not rope_ranges:
            o_ref[...] = acc_ref[...].astype(o_ref.dtype)
            return
        j = pl.program_id(1)
        plain = None
        for (lo, hi, t) in rope_ranges:
            hit = (j >= lo) & (j < hi)
            plain = hit if plain is None else (plain | hit)

            @pl.when(hit)
            def _(t=t):
                c = tab_refs[2 * t][...]
                s = tab_refs[2 * t + 1][...]
                for g in range(tn // LANES):
                    z = acc_ref[:, g * LANES:(g + 1) * LANES]
                    o_ref[:, g * LANES:(g + 1) * LANES] = _rope128(z, c, s, swaps[t]).astype(o_ref.dtype)

        @pl.when(jnp.logical_not(plain))
        def _():
            o_ref[...] = acc_ref[...].astype(o_ref.dtype)


def _matmul(a, b, *, tm, tn, tk, out_dtype, rope=None, name="mm"):
    M, K = a.shape
    _, N = b.shape
    tm, tn, tk = min(tm, M), min(tn, N), min(tk, K)
    assert M % tm == 0 and N % tn == 0 and K % tk == 0
    rope = rope or []
    ranges, tabs, swaps = [], [], []
    for t, (lo, hi, (c, s), swap) in enumerate(rope):
        assert lo % tn == 0 and hi % tn == 0 and tn % LANES == 0
        ranges.append((lo // tn, hi // tn, t))
        tabs += [c, s]
        swaps.append(swap)
    nk = K // tk
    in_specs = [pl.BlockSpec((tm, tk), lambda i, j, k: (i, k)),
                pl.BlockSpec((tk, tn), lambda i, j, k: (k, j))]
    in_specs += [pl.BlockSpec((tm, LANES), lambda i, j, k: (i, 0)) for _ in tabs]
    return pl.pallas_call(
        functools.partial(_mm_kernel, nk=nk, tn=tn, rope_ranges=tuple(ranges), swaps=tuple(swaps)),
        grid=(M // tm, N // tn, nk),
        in_specs=in_specs,
        out_specs=pl.BlockSpec((tm, tn), lambda i, j, k: (i, j)),
        out_shape=jax.ShapeDtypeStruct((M, N), out_dtype),
        scratch_shapes=[pltpu.VMEM((tm, tn), F32)],
        compiler_params=_cparams(("parallel", "parallel", "arbitrary")),
        name=name,
    )(a, b, *tabs)


def _softmax_step(s, v, m_ref, l_ref, acc_ref, scale):
    m_prev = m_ref[...]
    m_new = jnp.maximum(m_prev, jnp.max(s, axis=1, keepdims=True))
    alpha = jnp.exp((m_prev - m_new) * scale)
    p = jnp.exp((s - m_new) * scale)
    l_ref[...] = alpha * l_ref[...] + jnp.sum(p, axis=1, keepdims=True)
    acc_ref[...] = alpha * acc_ref[...] + _dot(p.astype(BF16), v)
    m_ref[...] = m_new


def _causal_mask(s, row0, col0):
    row = row0 + lax.broadcasted_iota(I32, s.shape, 0)
    col = col0 + lax.broadcasted_iota(I32, s.shape, 1)
    return jnp.where(col <= row, s, NEG)


def _moba_kernel(q_ref, k_ref, v_ref, o_ref, kmean_ref, m_ref, l_ref, acc_ref, *, T, scale):
    i = pl.program_id(1)
    tq = MOBA_BLOCK
    nb = T // MOBA_BLOCK

    @pl.when(i == 0)
    def _():
        blk = lax.broadcasted_iota(I32, (LANES, T), 0)
        pos = lax.broadcasted_iota(I32, (LANES, T), 1)
        avg = jnp.where(pos // MOBA_BLOCK == blk, 1.0 / MOBA_BLOCK, 0.0).astype(BF16)
        kmean_ref[...] = _dot(avg, k_ref[...])

    q = q_ref[...]
    gate = _dot_nt(q.astype(F32), kmean_ref[...], precision=lax.Precision.HIGHEST)
    lane = lax.broadcasted_iota(I32, (tq, LANES), 1)
    lane_f = lane.astype(F32)
    past = lane < i
    g = jnp.where(past, gate, -jnp.inf)
    sel = jnp.zeros((tq, LANES), jnp.bool_)
    for _ in range(min(MOBA_TOPK, nb)):
        mx = jnp.max(g, axis=1, keepdims=True)
        first = jnp.min(jnp.where(g == mx, lane_f, float(LANES)), axis=1, keepdims=True)
        pick = lane_f == first
        sel = sel | pick
        g = jnp.where(pick, -jnp.inf, g)
    sel = sel & past
    q_aug = jnp.concatenate([q, jnp.where(sel, 0.0, NEG).astype(BF16)], axis=1)

    m_ref[...] = jnp.full_like(m_ref, NEG)
    l_ref[...] = jnp.zeros_like(l_ref)
    acc_ref[...] = jnp.zeros_like(acc_ref)
    key_lane = lax.broadcasted_iota(I32, (MOBA_BLOCK, LANES), 1)

    def past_block(j, carry):
        rows = pl.ds(pl.multiple_of(j * MOBA_BLOCK, MOBA_BLOCK), MOBA_BLOCK)
        k_aug = jnp.concatenate([k_ref[rows, :], (key_lane == j).astype(BF16)], axis=1)
        _softmax_step(_dot_nt(q_aug, k_aug), v_ref[rows, :], m_ref, l_ref, acc_ref, scale)
        return carry

    lax.fori_loop(0, i, past_block, 0)

    rows = pl.ds(pl.multiple_of(i * MOBA_BLOCK, MOBA_BLOCK), MOBA_BLOCK)
    s = _causal_mask(_dot_nt(q, k_ref[rows, :]), 0, 0)
    _softmax_step(s, v_ref[rows, :], m_ref, l_ref, acc_ref, scale)
    o_ref[...] = (acc_ref[...] / l_ref[...]).astype(o_ref.dtype)


def _moba_attention(h0, T):
    assert T % MOBA_BLOCK == 0 and T // MOBA_BLOCK <= LANES
    H = MOBA_HEADS
    return pl.pallas_call(
        functools.partial(_moba_kernel, T=T, scale=HEAD_DIM ** -0.5),
        grid=(H, T // MOBA_BLOCK),
        in_specs=[pl.BlockSpec((MOBA_BLOCK, HEAD_DIM), lambda h, i: (i, h)),
                  pl.BlockSpec((T, HEAD_DIM), lambda h, i: (0, H + h)),
                  pl.BlockSpec((T, HEAD_DIM), lambda h, i: (0, 2 * H + h))],
        out_specs=pl.BlockSpec((MOBA_BLOCK, HEAD_DIM), lambda h, i: (i, h)),
        out_shape=jax.ShapeDtypeStruct((T, MOBA_W), BF16),
        scratch_shapes=[pltpu.VMEM((LANES, HEAD_DIM), F32),
                        pltpu.VMEM((MOBA_BLOCK, 1), F32), pltpu.VMEM((MOBA_BLOCK, 1), F32),
                        pltpu.VMEM((MOBA_BLOCK, HEAD_DIM), F32)],
        compiler_params=_cparams(("parallel", "arbitrary")),
        name="moba_attention",
    )(h0, h0, h0)


DIFF_TQ = 512


def _diff_kernel(lq1_ref, lk1_ref, lq2_ref, lk2_ref, g_ref, q_ref, k_ref, v_ref, o_ref,
                 m_ref, l_ref, acc_ref, *, tq, scale, lam_init):
    i = pl.program_id(1)
    m_ref[...] = jnp.full_like(m_ref, NEG)
    l_ref[...] = jnp.zeros_like(l_ref)
    acc_ref[...] = jnp.zeros_like(acc_ref)
    q = q_ref[...]

    def block(j, masked):
        rows = pl.ds(pl.multiple_of(j * tq, tq), tq)
        kj = k_ref[rows, :]
        vj = v_ref[rows, :]
        for mp in range(2):
            cols = slice(mp * HEAD_DIM, (mp + 1) * HEAD_DIM)
            s = _dot_nt(q[:, cols], kj[:, cols])
            if masked:
                s = _causal_mask(s, 0, 0)
            _softmax_step(s, vj, m_ref.at[mp], l_ref.at[mp], acc_ref.at[mp], scale)

    def past_block(j, carry):
        block(j, False)
        return carry

    lax.fori_loop(0, i, past_block, 0)
    block(i, True)

    lam = (jnp.exp(jnp.sum(lq1_ref[...] * lk1_ref[...], axis=1, keepdims=True))
           - jnp.exp(jnp.sum(lq2_ref[...] * lk2_ref[...], axis=1, keepdims=True)) + lam_init)
    o = acc_ref[0] / l_ref[0] - lam * (acc_ref[1] / l_ref[1])
    o = o * lax.rsqrt(jnp.mean(o * o, axis=1, keepdims=True) + 1e-5) * g_ref[...]
    o_ref[...] = (o * (1.0 - lam_init)).astype(o_ref.dtype)


def _diff_attention(h0, T, lq1, lk1, lq2, lk2, g_subln, layer_idx):
    tq = min(DIFF_TQ, T)
    assert T % tq == 0
    W = 2 * HEAD_DIM
    q0, k0, v0 = 3 * MOBA_W // W, (3 * MOBA_W + DIFF_W) // W, (3 * MOBA_W + 2 * DIFF_W) // W
    lam_init = 0.8 - 0.6 * math.exp(-0.3 * layer_idx)
    vec = lambda a: a.reshape(1, -1).astype(F32)
    small = lambda n: pl.BlockSpec((1, n), lambda h, i: (0, 0))
    return pl.pallas_call(
        functools.partial(_diff_kernel, tq=tq, scale=HEAD_DIM ** -0.5, lam_init=lam_init),
        grid=(DIFF_HEADS, T // tq),
        in_specs=[small(HEAD_DIM)] * 4 + [small(W),
                  pl.BlockSpec((tq, W), lambda h, i: (i, q0 + h)),
                  pl.BlockSpec((T, W), lambda h, i: (0, k0 + h)),
                  pl.BlockSpec((T, W), lambda h, i: (0, v0 + h))],
        out_specs=pl.BlockSpec((tq, W), lambda h, i: (i, h)),
        out_shape=jax.ShapeDtypeStruct((T, DIFF_W), BF16),
        scratch_shapes=[pltpu.VMEM((2, tq, 1), F32), pltpu.VMEM((2, tq, 1), F32),
                        pltpu.VMEM((2, tq, W), F32)],
        compiler_params=_cparams(("parallel", "arbitrary")),
        name="diff_attention",
    )(vec(lq1), vec(lk1), vec(lq2), vec(lk2), vec(g_subln), h0, h0, h0)


def _layernorm(v, g, b):
    mu = jnp.mean(v, axis=1, keepdims=True)
    d = v - mu
    var = jnp.mean(d * d, axis=1, keepdims=True)
    return d * lax.rsqrt(var + LN_EPS) * g + b


def _route(xn, wr_ref, br_ref, idx_ref, gate_ref):
    logits = jnp.dot(xn, wr_ref[...], preferred_element_type=F32,
                     precision=lax.Precision.HIGHEST) + br_ref[...]
    lane = lax.broadcasted_iota(I32, logits.shape, 1)
    lane_f = lane.astype(F32)
    idx_out = jnp.zeros(logits.shape, F32)
    val_out = jnp.full(logits.shape, NEG, F32)
    g = logits
    for r in range(TOP_K):
        mx = jnp.max(g, axis=1, keepdims=True)
        first = jnp.min(jnp.where(g == mx, lane_f, float(LANES)), axis=1, keepdims=True)
        idx_out = jnp.where(lane == r, first, idx_out)
        val_out = jnp.where(lane == r, mx, val_out)
        g = jnp.where(lane_f == first, -jnp.inf, g)
    e = jnp.exp(val_out - jnp.max(val_out, axis=1, keepdims=True))
    gate_ref[...] = e / jnp.sum(e, axis=1, keepdims=True)
    idx_ref[...] = idx_out.astype(I32)


def _mix_ln_kernel(x_ref, mix_ref, g_ref, b_ref, wr_ref, br_ref, xn_ref, xb_ref, idx_ref, gate_ref):
    xn = _layernorm(DN_ALPHA * x_ref[...] + mix_ref[...], g_ref[...], b_ref[...])
    xn_ref[...] = xn
    xb_ref[...] = xn.astype(BF16)
    _route(xn, wr_ref, br_ref, idx_ref, gate_ref)


def _mix_ln_route(x, mix, g, b, w_router, b_router, tm=256):
    T, D = x.shape
    E = w_router.shape[1]
    wr = jnp.pad(w_router.astype(F32), ((0, 0), (0, LANES - E)))
    br = jnp.pad(b_router.astype(F32).reshape(1, E), ((0, 0), (0, LANES - E)), constant_values=NEG)
    row = lambda n: pl.BlockSpec((tm, n), lambda i: (i, 0))
    const = lambda r, n: pl.BlockSpec((r, n), lambda i: (0, 0))
    return pl.pallas_call(
        _mix_ln_kernel,
        grid=(T // tm,),
        in_specs=[row(D), row(D), const(1, D), const(1, D), const(D, LANES), const(1, LANES)],
        out_specs=[row(D), row(D), row(LANES), row(LANES)],
        out_shape=[jax.ShapeDtypeStruct((T, D), F32), jax.ShapeDtypeStruct((T, D), BF16),
                   jax.ShapeDtypeStruct((T, LANES), I32), jax.ShapeDtypeStruct((T, LANES), F32)],
        compiler_params=_cparams(("parallel",)),
        name="mix_ln_route",
    )(x, mix, g.reshape(1, D).astype(F32), b.reshape(1, D).astype(F32), wr, br)


def _combine_ln_kernel(x_ref, y_ref, gate_ref, g_ref, b_ref, xn_ref, xb_ref, *, D):
    gates = gate_ref[...]
    f = jnp.zeros(x_ref.shape, F32)
    for r in range(TOP_K):
        f = f + gates[:, r:r + 1] * y_ref[:, r * D:(r + 1) * D].astype(F32)
    xn = _layernorm(DN_ALPHA * x_ref[...] + f, g_ref[...], b_ref[...])
    xn_ref[...] = xn
    xb_ref[...] = xn.astype(BF16)


def _combine_ln(x, y4, gates, g, b, tm=256):
    T, D = x.shape
    row = lambda n: pl.BlockSpec((tm, n), lambda i: (i, 0))
    const = lambda r, n: pl.BlockSpec((r, n), lambda i: (0, 0))
    return pl.pallas_call(
        functools.partial(_combine_ln_kernel, D=D),
        grid=(T // tm,),
        in_specs=[row(D), row(TOP_K * D), row(LANES), const(1, D), const(1, D)],
        out_specs=[row(D), row(D)],
        out_shape=[jax.ShapeDtypeStruct((T, D), F32), jax.ShapeDtypeStruct((T, D), BF16)],
        compiler_params=_cparams(("parallel",)),
        name="moe_combine_ln",
    )(x, y4, gates, g.reshape(1, D).astype(F32), b.reshape(1, D).astype(F32))


MOE_TM = 512
MOE_FC = 256


def _expert_kernel(te_ref, na_ref, x_ref, wgu_ref, bgu_ref, wd_ref, bd_ref, sel_ref, y_ref, acc_ref, *, nc):
    i = pl.program_id(0)
    c = pl.program_id(1)

    @pl.when(i < na_ref[0])
    def _():
        @pl.when(c == 0)
        def _():
            acc_ref[...] = jnp.zeros_like(acc_ref)

        hg = _dot(x_ref[...], wgu_ref[...].astype(BF16)) + bgu_ref[...]
        gate = jnp.minimum(hg, SWIGLU_LIMIT)
        up = jnp.clip(hg, -SWIGLU_LIMIT, SWIGLU_LIMIT)
        up_next = pltpu.roll(up, 2 * MOE_FC - 1, axis=1)
        act = (up_next + 1.0) * (gate * (1.0 / (1.0 + jnp.exp(-SWIGLU_ALPHA * gate))))
        act = _dot(act.astype(BF16), sel_ref[...]).astype(BF16)
        acc_ref[...] += _dot(act, wd_ref[...].astype(BF16))

        @pl.when(c == nc - 1)
        def _():
            y_ref[...] = (acc_ref[...] + bd_ref[...]).astype(y_ref.dtype)

    @pl.when((i >= na_ref[0]) & (c == nc - 1))
    def _():
        y_ref[...] = jnp.zeros_like(y_ref)


def _experts(xs, tile_expert, n_active, w_gu, b_gu, w_down, b_down):
    P, D = xs.shape
    E, _, F2 = w_gu.shape
    F = F2 // 2
    tm, fc = MOE_TM, min(MOE_FC, F)
    nc = F // fc
    nt = P // tm
    even = (lax.broadcasted_iota(I32, (2 * fc, fc), 0) == 2 * lax.broadcasted_iota(I32, (2 * fc, fc), 1)).astype(BF16)

    def live(i, na):
        return jnp.minimum(i, jnp.maximum(na[0], 1) - 1)

    def chunk(i, c, na):
        return jnp.where(i < na[0], c, nc - 1)

    grid_spec = pltpu.PrefetchScalarGridSpec(
        num_scalar_prefetch=2,
        grid=(nt, nc),
        in_specs=[
            pl.BlockSpec((tm, D), lambda i, c, te, na: (live(i, na), 0)),
            pl.BlockSpec((None, D, 2 * fc), lambda i, c, te, na: (te[live(i, na)], 0, chunk(i, c, na))),
            pl.BlockSpec((None, 1, 2 * fc), lambda i, c, te, na: (te[live(i, na)], 0, chunk(i, c, na))),
            pl.BlockSpec((None, fc, D), lambda i, c, te, na: (te[live(i, na)], chunk(i, c, na), 0)),
            pl.BlockSpec((None, 1, D), lambda i, c, te, na: (te[live(i, na)], 0, 0)),
            pl.BlockSpec((2 * fc, fc), lambda i, c, te, na: (0, 0)),
        ],
        out_specs=pl.BlockSpec((tm, D), lambda i, c, te, na: (i, 0)),
        scratch_shapes=[pltpu.VMEM((tm, D), F32)],
    )
    return pl.pallas_call(
        functools.partial(_expert_kernel, nc=nc),
        grid_spec=grid_spec,
        out_shape=jax.ShapeDtypeStruct((P, D), BF16),
        compiler_params=_cparams(("arbitrary", "arbitrary")),
        name="moe_experts",
    )(tile_expert, n_active, xs, w_gu, b_gu.reshape(E, 1, F2), w_down, b_down.reshape(E, 1, D), even)


def _moe(xn, xb, top_idx, gates, w_gu, b_gu, w_down, b_down, ln_g, ln_b):
    T, D = xn.shape
    E = w_gu.shape[0]
    tm = MOE_TM
    e_flat = top_idx[:, :TOP_K].reshape(-1)
    onehot = (e_flat[:, None] == jnp.arange(E, dtype=I32)[None, :]).astype(I32)
    rank = jnp.take_along_axis(jnp.cumsum(onehot, axis=0), e_flat[:, None], axis=1)[:, 0] - 1
    counts = jnp.sum(onehot, axis=0)
    padded = ((counts + tm - 1) // tm) * tm
    ends = jnp.cumsum(padded)
    starts = ends - padded
    slot = starts[e_flat] + rank
    n_tiles = (T * TOP_K + E * (tm - 1)) // tm
    P = n_tiles * tm
    token_of_slot = jnp.zeros((P,), I32).at[slot].set(jnp.arange(T * TOP_K, dtype=I32) // TOP_K)
    tile_expert = jnp.minimum(jnp.searchsorted(ends, jnp.arange(n_tiles, dtype=I32) * tm, side="right"),
                              E - 1).astype(I32)
    n_active = (ends[-1:] // tm).astype(I32)

    xs = jnp.take(xb, token_of_slot, axis=0)
    y = _experts(xs, tile_expert, n_active, w_gu, b_gu, w_down, b_down)
    y4 = jnp.take(y, slot, axis=0).reshape(T, TOP_K * D)
    return _combine_ln(xn, y4, gates, ln_g, ln_b)


L1_IN_PAD = 1792


def _l1_post_kernel(h_ref, gq_ref, gkv_ref, gk_ref, bk_ref, c_ref, s_ref, cq_ref, kc_ref, ki_ref, wi_ref):
    def rms(v, g):
        return v * lax.rsqrt(jnp.mean(v * v, axis=1, keepdims=True) + 1e-6) * g

    cq_ref[...] = rms(h_ref[:, :Q_LORA], gq_ref[...]).astype(BF16)
    kc_ref[:, :KV_LORA] = rms(h_ref[:, Q_LORA:Q_LORA + KV_LORA], gkv_ref[...]).astype(BF16)
    c, s = c_ref[...], s_ref[...]
    o_idx = Q_LORA + KV_LORA
    ki = _layernorm(h_ref[:, o_idx:o_idx + IDX_DIM], gk_ref[...], bk_ref[...])
    ki_ref[...] = _rope128(ki, c, s, _swap_pair).astype(BF16)
    tail = h_ref[:, o_idx + IDX_DIM:]
    kc_ref[:, KV_LORA:] = _rope128(tail, c, s, _swap_pair)[:, :QK_ROPE].astype(BF16)
    wi_ref[...] = tail * (IDX_HEADS ** -0.5 * IDX_DIM ** -0.5)


def _l1_post(h1, g_q, g_kv, g_kidx, b_kidx, idx_tab, tm=256):
    T = h1.shape[0]
    row = lambda n: pl.BlockSpec((tm, n), lambda i: (i, 0))
    const = lambda n: pl.BlockSpec((1, n), lambda i: (0, 0))
    vec = lambda a: a.reshape(1, -1).astype(F32)
    return pl.pallas_call(
        _l1_post_kernel,
        grid=(T // tm,),
        in_specs=[row(L1_IN_PAD), const(Q_LORA), const(KV_LORA), const(IDX_DIM), const(IDX_DIM),
                  row(LANES), row(LANES)],
        out_specs=[row(Q_LORA), row(KV_LORA + QK_ROPE), row(IDX_DIM), row(LANES)],
        out_shape=[jax.ShapeDtypeStruct((T, Q_LORA), BF16), jax.ShapeDtypeStruct((T, KV_LORA + QK_ROPE), BF16),
                   jax.ShapeDtypeStruct((T, IDX_DIM), BF16), jax.ShapeDtypeStruct((T, LANES), F32)],
        compiler_params=_cparams(("parallel",)),
        name="l1_post",
    )(h1, vec(g_q), vec(g_kv), vec(g_kidx), vec(b_kidx), idx_tab[0], idx_tab[1])


def _qlat_kernel(qn_ref, wuk_ref, qpe_ref, o_ref):
    ql = _dot_nt(qn_ref[...], wuk_ref[...].astype(BF16))
    o_ref[:, :KV_LORA] = ql.astype(BF16)
    o_ref[:, KV_LORA:] = qpe_ref[...]


def _q_absorb(q_nope, w_uk, q_pe_h, tm=1024):
    T = q_nope.shape[0]
    tm = min(tm, T)
    H = MLA_HEADS
    return pl.pallas_call(
        _qlat_kernel,
        grid=(H, T // tm),
        in_specs=[pl.BlockSpec((tm, QK_NOPE), lambda h, i: (i, h)),
                  pl.BlockSpec((KV_LORA, QK_NOPE), lambda h, i: (0, h)),
                  pl.BlockSpec((None, tm, QK_ROPE), lambda h, i: (h, i, 0))],
        out_specs=pl.BlockSpec((None, tm, KV_LORA + QK_ROPE), lambda h, i: (h, i, 0)),
        out_shape=jax.ShapeDtypeStruct((H, T, KV_LORA + QK_ROPE), BF16),
        compiler_params=_cparams(("parallel", "parallel")),
        name="q_absorb",
    )(q_nope, w_uk, q_pe_h)


DSA_TQ = 64
DSA_KB = 256
DSA_TK = 512


def _indexer_kernel(qi_ref, w_ref, ki_ref, bias_ref, key_ref, *, T, tq, n_sel):
    i = pl.program_id(0)
    kb = DSA_KB
    nkb = T // kb
    n_live = ((i + 1) * tq + kb - 1) // kb
    q = qi_ref[...].reshape(IDX_HEADS * tq, IDX_DIM)
    w = w_ref[...]

    def score_block(j, carry):
        rows = pl.ds(pl.multiple_of(j * kb, kb), kb)
        s = _dot_nt(q, ki_ref[rows, :])
        isc = jnp.sum(jnp.maximum(s, 0.0).reshape(IDX_HEADS, tq, kb) * w, axis=0) + 0.0
        qpos = i * tq + lax.broadcasted_iota(I32, (tq, kb), 0)
        kpos = j * kb + lax.broadcasted_iota(I32, (tq, kb), 1)
        isc = jnp.where(kpos <= qpos, isc, -jnp.inf)
        bits = pltpu.bitcast(isc, I32)
        key_ref[j] = jnp.where(bits < 0, bits ^ 0x7FFFFFFF, bits)
        return carry

    lax.fori_loop(0, n_live, score_block, 0)

    def count_ge(cand):
        cb = jnp.broadcast_to(cand, (tq, LANES))

        def body(j, c):
            blk = key_ref[j]
            for g in range(kb // LANES):
                c = c + (blk[:, g * LANES:(g + 1) * LANES] >= cb).astype(F32)
            return c

        c = lax.fori_loop(0, n_live, body, jnp.zeros((tq, LANES), F32))
        return jnp.sum(c, axis=1, keepdims=True)

    thr = jnp.full((tq, 1), -2**31, I32)
    for bit in range(31, -1, -1):
        cand = jnp.zeros((tq, 1), I32) if bit == 31 else thr + (1 << bit)
        thr = jnp.where(count_ge(cand) >= n_sel, cand, thr)

    def write_block(j, carry):
        blk = key_ref[j]
        bias_ref[j] = jnp.where((blk >= thr) & (blk > KEY_NEG_INF), 0.0, NEG).astype(BF16)
        return carry

    lax.fori_loop(0, n_live, write_block, 0)

    def dead_block(j, carry):
        bias_ref[j] = jnp.full((tq, kb), NEG, BF16)
        return carry

    lax.fori_loop(n_live, nkb, dead_block, 0)


def _indexer_mask(qi_h, w_col, ki, T, n_sel):
    tq, kb = min(DSA_TQ, T), DSA_KB
    nkb = T // kb
    return pl.pallas_call(
        functools.partial(_indexer_kernel, T=T, tq=tq, n_sel=n_sel),
        grid=(T // tq,),
        in_specs=[pl.BlockSpec((IDX_HEADS, tq, IDX_DIM), lambda i: (0, i, 0)),
                  pl.BlockSpec((IDX_HEADS, tq, 1), lambda i: (0, i, 0)),
                  pl.BlockSpec((T, IDX_DIM), lambda i: (0, 0), pipeline_mode=pl.Buffered(1))],
        out_specs=pl.BlockSpec((None, nkb, tq, kb), lambda i: (i, 0, 0, 0)),
        out_shape=jax.ShapeDtypeStruct((T // tq, nkb, tq, kb), BF16),
        scratch_shapes=[pltpu.VMEM((nkb, tq, kb), I32)],
        compiler_params=_cparams(("parallel",)),
        name="dsa_indexer",
    )(qi_h, w_col, ki)


def _dsa_kernel(q_ref, bias_ref, kc_ref, wuv_ref, o_ref, s_ref, p_ref, a_ref, m_ref, l_ref, acc_ref,
                *, tq, scale):
    i = pl.program_id(0)
    H = MLA_HEADS
    tk = DSA_TK
    per = tk // DSA_KB
    n_steps = ((i + 1) * tq + tk - 1) // tk
    q = q_ref[...].reshape(H * tq, KV_LORA + QK_ROPE)
    m_ref[...] = jnp.full_like(m_ref, NEG)
    l_ref[...] = jnp.zeros_like(l_ref)
    acc_ref[...] = jnp.zeros_like(acc_ref)

    def step(j, carry):
        rows = pl.ds(pl.multiple_of(j * tk, tk), tk)
        kc = kc_ref[rows, :]
        s_ref[...] = _dot_nt(q, kc)
        bias = jnp.concatenate([bias_ref[j * per + b] for b in range(per)], axis=1).astype(F32)

        def head(h, c):
            r = pl.ds(pl.multiple_of(h * tq, tq), tq)
            s = s_ref[r, :] + bias
            m_prev = m_ref[r, :]
            m_new = jnp.maximum(m_prev, jnp.max(s, axis=1, keepdims=True))
            alpha = jnp.exp((m_prev - m_new) * scale)
            p = jnp.exp((s - m_new) * scale)
            l_ref[r, :] = alpha * l_ref[r, :] + jnp.sum(p, axis=1, keepdims=True)
            m_ref[r, :] = m_new
            a_ref[r, :] = alpha
            p_ref[r, :] = p.astype(BF16)
            return c

        lax.fori_loop(0, H, head, 0)
        acc_ref[...] = a_ref[...] * acc_ref[...] + _dot(p_ref[...], kc[:, :KV_LORA])
        return carry

    lax.fori_loop(0, n_steps, step, 0)

    for h in range(H):
        r = slice(h * tq, (h + 1) * tq)
        o_lat = (acc_ref[r, :] / l_ref[r, :]).astype(BF16)
        o_ref[:, h * V_HEAD:(h + 1) * V_HEAD] = _dot(o_lat, wuv_ref[h]).astype(o_ref.dtype)


def _dsa_attention(q576, bias, kc, w_uv_h, T):
    tq = min(DSA_TQ, T)
    H = MLA_HEADS
    C = KV_LORA + QK_ROPE
    nkb = T // DSA_KB
    assert T % DSA_TK == 0
    return pl.pallas_call(
        functools.partial(_dsa_kernel, tq=tq, scale=(QK_NOPE + QK_ROPE) ** -0.5),
        grid=(T // tq,),
        in_specs=[pl.BlockSpec((H, tq, C), lambda i: (0, i, 0)),
                  pl.BlockSpec((None, nkb, tq, DSA_KB), lambda i: (i, 0, 0, 0)),
                  pl.BlockSpec((T, C), lambda i: (0, 0), pipeline_mode=pl.Buffered(1)),
                  pl.BlockSpec((H, KV_LORA, V_HEAD), lambda i: (0, 0, 0), pipeline_mode=pl.Buffered(1))],
        out_specs=pl.BlockSpec((tq, H * V_HEAD), lambda i: (i, 0)),
        out_shape=jax.ShapeDtypeStruct((T, H * V_HEAD), BF16),
        scratch_shapes=[pltpu.VMEM((H * tq, DSA_TK), F32), pltpu.VMEM((H * tq, DSA_TK), BF16),
                        pltpu.VMEM((H * tq, 1), F32), pltpu.VMEM((H * tq, 1), F32), pltpu.VMEM((H * tq, 1), F32),
                        pltpu.VMEM((H * tq, KV_LORA), F32)],
        compiler_params=_cparams(("parallel",)),
        name="dsa_attention",
    )(q576, bias, kc, w_uv_h)


def _even_mixer(xb, T, w_in, w_out, lq1, lk1, lq2, lk2, g_subln, layer_idx, tabs):
    full = tabs[0]
    rope = [(0, 2 * MOBA_W, full, _swap_full),
            (3 * MOBA_W, 3 * MOBA_W + 2 * DIFF_W, full, _swap_full)]
    h0 = _matmul(xb, w_in, tm=2048, tn=1024, tk=512, out_dtype=BF16, rope=rope, name="l0_in_proj")
    o_moba = _moba_attention(h0, T)
    o_diff = _diff_attention(h0, T, lq1, lk1, lq2, lk2, g_subln, layer_idx)
    o = jnp.concatenate([o_moba, o_diff], axis=1)
    return _matmul(o, w_out, tm=2048, tn=1024, tk=512, out_dtype=F32, name="l0_out_proj")


def _odd_mixer(xb, T, w_in, g_q, g_kv, w_qb, w_uk, w_uv, w_iq, g_kidx, b_kidx, w_out, tabs):
    _, pair, idx = tabs
    D = w_in.shape[0]
    H = MLA_HEADS
    o_kpe, o_kidx, o_w = Q_LORA + KV_LORA, Q_LORA + KV_LORA + QK_ROPE, Q_LORA + KV_LORA + QK_ROPE + IDX_DIM
    w_in_p = jnp.concatenate([w_in[:, :o_kpe], w_in[:, o_kidx:o_w], w_in[:, o_kpe:o_kidx], w_in[:, o_w:],
                              jnp.zeros((D, L1_IN_PAD - w_in.shape[1]), w_in.dtype)], axis=1)
    h1 = _matmul(xb, w_in_p, tm=2048, tn=L1_IN_PAD // 2, tk=512, out_dtype=F32, name="l1_in_proj")
    cq, kc, ki, wi_full = _l1_post(h1, g_q, g_kv, g_kidx, b_kidx, idx)
    w_qb3 = w_qb.reshape(Q_LORA, H, QK_NOPE + QK_ROPE)
    w_qb_p = jnp.concatenate([w_qb3[:, :, :QK_NOPE].reshape(Q_LORA, H * QK_NOPE),
                              w_qb3[:, :, QK_NOPE:].reshape(Q_LORA, H * QK_ROPE)], axis=1)
    n0 = H * QK_NOPE
    q = _matmul(cq, w_qb_p, tm=2048, tn=1024, tk=Q_LORA, out_dtype=BF16,
                rope=[(n0, n0 + H * QK_ROPE, pair, _swap_pair)], name="l1_q_proj")
    qi = _matmul(cq, w_iq, tm=2048, tn=1024, tk=Q_LORA, out_dtype=BF16,
                 rope=[(0, IDX_HEADS * IDX_DIM, idx, _swap_pair)], name="l1_qi_proj")
    q_pe_h = q[:, n0:].reshape(T, H, QK_ROPE).transpose(1, 0, 2)
    q576 = _q_absorb(q, w_uk, q_pe_h)
    qi_h = qi.reshape(T, IDX_HEADS, IDX_DIM).transpose(1, 0, 2)
    w_col = wi_full[:, QK_ROPE:QK_ROPE + IDX_HEADS].T[:, :, None]
    bias = _indexer_mask(qi_h, w_col, ki, T, min(IDX_TOPK_MAX, T // 4))
    w_uv_h = w_uv.reshape(KV_LORA, H, V_HEAD).transpose(1, 0, 2).astype(BF16)
    o = _dsa_attention(q576, bias, kc, w_uv_h, T)
    return _matmul(o, w_out, tm=2048, tn=1024, tk=512, out_dtype=F32, name="l1_out_proj")


def kernel(x, l0_w_in, l0_w_out, l0_lam_q1, l0_lam_k1, l0_lam_q2, l0_lam_k2, l0_g_subln, l1_w_in, l1_g_q, l1_g_kv, l1_w_qb, l1_w_uk, l1_w_uv, l1_w_iq, l1_g_kidx, l1_b_kidx, l1_w_out, ln_g, ln_b, moe_w_router, moe_b_router, moe_w_gu, moe_b_gu, moe_w_down, moe_b_down):
    B, T, D = x.shape
    assert B == 1
    xn = x.reshape(T, D)
    xb = xn.astype(BF16)
    tabs = _rope_tables(T)
    for i in range(DEPTH):
        if i % 2 == 0:
            mix = _even_mixer(xb, T, l0_w_in, l0_w_out, l0_lam_q1, l0_lam_k1, l0_lam_q2, l0_lam_k2,
                              l0_g_subln, i, tabs)
        else:
            mix = _odd_mixer(xb, T, l1_w_in, l1_g_q, l1_g_kv, l1_w_qb, l1_w_uk, l1_w_uv, l1_w_iq,
                             l1_g_kidx, l1_b_kidx, l1_w_out, tabs)
        xn, xb, top_idx, gates = _mix_ln_route(xn, mix, ln_g[i, 0], ln_b[i, 0], moe_w_router[i], moe_b_router[i])
        xn, xb = _moe(xn, xb, top_idx, gates, moe_w_gu[i], moe_b_gu[i], moe_w_down[i], moe_b_down[i],
                      ln_g[i, 1], ln_b[i, 1])
    return xn.reshape(B, T, D)
```

```python
import functools
import math

import jax
import jax.numpy as jnp
from jax import lax
from jax.experimental import pallas as pl
from jax.experimental.pallas import tpu as pltpu

F32 = jnp.float32
BF16 = jnp.bfloat16
I32 = jnp.int32

HEAD_DIM = 128
ROPE_THETA = 10000.0
MOBA_HEADS = 16
MOBA_BLOCK = 256
MOBA_TOPK = 3
DIFF_HEADS = 8
MOBA_W = MOBA_HEADS * HEAD_DIM
DIFF_W = DIFF_HEADS * 2 * HEAD_DIM
MLA_HEADS = 32
Q_LORA = 1024
KV_LORA = 512
QK_NOPE = 128
QK_ROPE = 64
V_HEAD = 128
IDX_HEADS = 32
IDX_DIM = 128
IDX_TOPK_MAX = 256
N_EXPERTS = 32
TOP_K = 4
SWIGLU_LIMIT = 7.0
SWIGLU_ALPHA = 1.702
DEPTH = 2
DN_ALPHA = (2 * DEPTH) ** 0.25
LN_EPS = 1e-5

LANES = 128
VMEM_LIMIT_BYTES = 56 * 2**20

NEG = -1e30
LOG2E = 1.4426950408889634
KEY_NEG_INF = -2139095041


def _cparams(semantics, flags=None):
    return pltpu.CompilerParams(dimension_semantics=semantics, vmem_limit_bytes=VMEM_LIMIT_BYTES, flags=flags)


def _dot(a, b):
    return jnp.dot(a, b, preferred_element_type=F32)


def _dot_nt(a, b, precision=None):
    return lax.dot_general(a, b, (((1,), (1,)), ((), ())), preferred_element_type=F32, precision=precision)


def _rope_tables(T):
    pos = jnp.arange(T).astype(F32)[:, None]
    inv64 = ROPE_THETA ** (-jnp.arange(64, dtype=F32) / 64)
    a64 = pos * inv64[None, :]
    c64, s64 = jnp.cos(a64), jnp.sin(a64)
    inv32 = ROPE_THETA ** (-jnp.arange(32, dtype=F32) / 32)
    a32 = pos * inv32[None, :]
    c32, s32 = jnp.cos(a32), jnp.sin(a32)
    one, zero = jnp.ones_like(c32), jnp.zeros_like(c32)
    full = (jnp.concatenate([c64, c64], 1), jnp.concatenate([-s64, s64], 1))
    pair = (jnp.concatenate([c32, c32, c32, c32], 1), jnp.concatenate([-s32, s32, -s32, s32], 1))
    idx = (jnp.concatenate([c32, c32, one, one], 1), jnp.concatenate([-s32, s32, zero, zero], 1))
    return full, pair, idx


def _swap_full(z):
    return pltpu.roll(z, 64, axis=1)


def _swap_pair(z):
    lane = lax.broadcasted_iota(I32, z.shape, 1)
    return jnp.where((lane % 64) < 32, pltpu.roll(z, 96, axis=1), pltpu.roll(z, 32, axis=1))


def _rope128(z, c, s, swap):
    return z * c + swap(z) * s


def _mm_kernel(*refs, nk, tn, rope_ranges, swaps):
    n_tab = len(swaps)
    a_ref, b_ref = refs[0], refs[1]
    tab_refs = refs[2:2 + 2 * n_tab]
    o_ref, acc_ref = refs[2 + 2 * n_tab], refs[3 + 2 * n_tab]
    k = pl.program_id(2)

    @pl.when(k == 0)
    def _():
        acc_ref[...] = jnp.zeros_like(acc_ref)

    acc_ref[...] += _dot(a_ref[...].astype(BF16), b_ref[...].astype(BF16))

    @pl.when(k == nk - 1)
    def _():
        if not rope_ranges:
            o_ref[...] = acc_ref[...].astype(o_ref.dtype)
            return
        j = pl.program_id(1)
        plain = None
        for (lo, hi, t) in rope_ranges:
            hit = (j >= lo) & (j < hi)
            plain = hit if plain is None else (plain | hit)

            @pl.when(hit)
            def _(t=t):
                c = tab_refs[2 * t][...]
                s = tab_refs[2 * t + 1][...]
                for g in range(tn // LANES):
                    z = acc_ref[:, g * LANES:(g + 1) * LANES]
                    o_ref[:, g * LANES:(g + 1) * LANES] = _rope128(z, c, s, swaps[t]).astype(o_ref.dtype)

        @pl.when(jnp.logical_not(plain))
        def _():
            o_ref[...] = acc_ref[...].astype(o_ref.dtype)


def _matmul(a, b, *, tm, tn, tk, out_dtype, rope=None, name="mm"):
    M, K = a.shape
    _, N = b.shape
    tm, tn, tk = min(tm, M), min(tn, N), min(tk, K)
    assert M % tm == 0 and N % tn == 0 and K % tk == 0
    rope = rope or []
    ranges, tabs, swaps = [], [], []
    for t, (lo, hi, (c, s), swap) in enumerate(rope):
        assert lo % tn == 0 and hi % tn == 0 and tn % LANES == 0
        ranges.append((lo // tn, hi // tn, t))
        tabs += [c, s]
        swaps.append(swap)
    nk = K // tk
    in_specs = [pl.BlockSpec((tm, tk), lambda i, j, k: (i, k)),
                pl.BlockSpec((tk, tn), lambda i, j, k: (k, j))]
    in_specs += [pl.BlockSpec((tm, LANES), lambda i, j, k: (i, 0)) for _ in tabs]
    return pl.pallas_call(
        functools.partial(_mm_kernel, nk=nk, tn=tn, rope_ranges=tuple(ranges), swaps=tuple(swaps)),
        grid=(M // tm, N // tn, nk),
        in_specs=in_specs,
        out_specs=pl.BlockSpec((tm, tn), lambda i, j, k: (i, j)),
        out_shape=jax.ShapeDtypeStruct((M, N), out_dtype),
        scratch_shapes=[pltpu.VMEM((tm, tn), F32)],
        compiler_params=_cparams(("parallel", "parallel", "arbitrary")),
        name=name,
    )(a, b, *tabs)


SOFTMAX_ROWS = 64


def _softmax_step(s, v, m_ref, l_ref, acc_ref, scale):
    c = scale * LOG2E
    tq, tk = s.shape
    dv = v.shape[1]
    rows = [slice(r, r + SOFTMAX_ROWS) for r in range(0, tq, SOFTMAX_ROWS)]
    p_rows, alphas = [], []
    for r in rows:
        sc = s[r, :]
        m_prev = m_ref[r, :]
        m_new = jnp.maximum(m_prev, jnp.max(sc, axis=1, keepdims=True))
        alpha = jnp.exp2((m_prev - m_new) * c)
        p = [jnp.exp2((sc[:, t:t + LANES] - m_new) * c) for t in range(0, tk, LANES)]
        l_ref[r, :] = alpha * l_ref[r, :] + sum(p[1:], p[0])
        m_ref[r, :] = m_new
        p_rows.append(jnp.concatenate([pt.astype(BF16) for pt in p], axis=1))
        alphas.append(alpha)
    pv = _dot(jnp.concatenate(p_rows, axis=0), v)
    for r, alpha in zip(rows, alphas):
        for t in range(0, dv, LANES):
            acc_ref[r, t:t + LANES] = alpha * acc_ref[r, t:t + LANES] + pv[r, t:t + LANES]


def _row_sum(l_ref):
    return jnp.sum(l_ref[...], axis=1, keepdims=True)


def _causal_mask(s, row0, col0):
    row = row0 + lax.broadcasted_iota(I32, s.shape, 0)
    col = col0 + lax.broadcasted_iota(I32, s.shape, 1)
    return jnp.where(col <= row, s, NEG)


MOBA_TILE = 512


def _moba_kernel(q_ref, k_ref, v_ref, o_ref, kmean_ref, m_ref, l_ref, acc_ref, *, T, tq, scale):
    i = pl.program_id(1)
    nb = T // MOBA_BLOCK
    per = tq // MOBA_BLOCK

    @pl.when(i == 0)
    def _():
        blk = lax.broadcasted_iota(I32, (LANES, T), 0)
        pos = lax.broadcasted_iota(I32, (LANES, T), 1)
        avg = jnp.where(pos // MOBA_BLOCK == blk, 1.0 / MOBA_BLOCK, 0.0).astype(BF16)
        kmean_ref[...] = _dot(avg, k_ref[...])

    q = q_ref[...]
    gate = _dot_nt(q.astype(F32), kmean_ref[...], precision=lax.Precision.HIGHEST)
    lane = lax.broadcasted_iota(I32, (tq, LANES), 1)
    lane_f = lane.astype(F32)
    own = i * per + lax.broadcasted_iota(I32, (tq, LANES), 0) // MOBA_BLOCK
    past = lane < own
    g = jnp.where(past, gate, -jnp.inf)
    sel = jnp.zeros((tq, LANES), jnp.bool_)
    for _ in range(min(MOBA_TOPK, nb)):
        mx = jnp.max(g, axis=1, keepdims=True)
        first = jnp.min(jnp.where(g == mx, lane_f, float(LANES)), axis=1, keepdims=True)
        pick = lane_f == first
        sel = sel | pick
        g = jnp.where(pick, -jnp.inf, g)
    visible = (sel & past) | (lane == own)
    q_aug = jnp.concatenate([q, jnp.where(visible, 0.0, NEG).astype(BF16)], axis=1)

    m_ref[...] = jnp.full_like(m_ref, NEG)
    l_ref[...] = jnp.zeros_like(l_ref)
    acc_ref[...] = jnp.zeros_like(acc_ref)
    key_blk = lax.broadcasted_iota(I32, (tq, LANES), 0) // MOBA_BLOCK
    key_lane = lax.broadcasted_iota(I32, (tq, LANES), 1)

    def tile(j, causal):
        rows = pl.ds(pl.multiple_of(j * tq, tq), tq)
        k_aug = jnp.concatenate([k_ref[rows, :], (key_lane == j * per + key_blk).astype(BF16)], axis=1)
        s = _dot_nt(q_aug, k_aug)
        if causal:
            s = _causal_mask(s, 0, 0)
        _softmax_step(s, v_ref[rows, :], m_ref, l_ref, acc_ref, scale)

    def past_tile(j, carry):
        tile(j, False)
        return carry

    lax.fori_loop(0, i, past_tile, 0)
    tile(i, True)
    o_ref[...] = (acc_ref[...] / _row_sum(l_ref)).astype(o_ref.dtype)


def _moba_attention(h0, T):
    tq = min(MOBA_TILE, T)
    assert T % tq == 0 and tq % MOBA_BLOCK == 0 and T // MOBA_BLOCK <= LANES
    H = MOBA_HEADS
    return pl.pallas_call(
        functools.partial(_moba_kernel, T=T, tq=tq, scale=HEAD_DIM ** -0.5),
        grid=(H, T // tq),
        in_specs=[pl.BlockSpec((tq, HEAD_DIM), lambda h, i: (i, h)),
                  pl.BlockSpec((T, HEAD_DIM), lambda h, i: (0, H + h)),
                  pl.BlockSpec((T, HEAD_DIM), lambda h, i: (0, 2 * H + h))],
        out_specs=pl.BlockSpec((tq, HEAD_DIM), lambda h, i: (i, h)),
        out_shape=jax.ShapeDtypeStruct((T, MOBA_W), BF16),
        scratch_shapes=[pltpu.VMEM((LANES, HEAD_DIM), F32),
                        pltpu.VMEM((tq, LANES), F32), pltpu.VMEM((tq, LANES), F32),
                        pltpu.VMEM((tq, HEAD_DIM), F32)],
        compiler_params=_cparams(("parallel", "arbitrary")),
        name="moba_attention",
    )(h0, h0, h0)


DIFF_TQ = 512


def _diff_kernel(lq1_ref, lk1_ref, lq2_ref, lk2_ref, g_ref, q_ref, k_ref, v_ref, o_ref,
                 m_ref, l_ref, acc_ref, *, tq, scale, lam_init):
    i = pl.program_id(1)
    m_ref[...] = jnp.full_like(m_ref, NEG)
    l_ref[...] = jnp.zeros_like(l_ref)
    acc_ref[...] = jnp.zeros_like(acc_ref)
    q = q_ref[...]

    def block(j, masked):
        rows = pl.ds(pl.multiple_of(j * tq, tq), tq)
        kj = k_ref[rows, :]
        vj = v_ref[rows, :]
        cols = [slice(mp * HEAD_DIM, (mp + 1) * HEAD_DIM) for mp in range(2)]
        scores = [_dot_nt(q[:, c], kj[:, c]) for c in cols]
        for mp in range(2):
            s = _causal_mask(scores[mp], 0, 0) if masked else scores[mp]
            _softmax_step(s, vj, m_ref.at[mp], l_ref.at[mp], acc_ref.at[mp], scale)

    def past_block(j, carry):
        block(j, False)
        return carry

    lax.fori_loop(0, i, past_block, 0)
    block(i, True)

    lam = (jnp.exp(jnp.sum(lq1_ref[...] * lk1_ref[...], axis=1, keepdims=True))
           - jnp.exp(jnp.sum(lq2_ref[...] * lk2_ref[...], axis=1, keepdims=True)) + lam_init)
    o = acc_ref[0] / _row_sum(l_ref.at[0]) - lam * (acc_ref[1] / _row_sum(l_ref.at[1]))
    o = o * lax.rsqrt(jnp.mean(o * o, axis=1, keepdims=True) + 1e-5) * g_ref[...]
    o_ref[...] = (o * (1.0 - lam_init)).astype(o_ref.dtype)


def _diff_attention(h0, T, lq1, lk1, lq2, lk2, g_subln, layer_idx):
    tq = min(DIFF_TQ, T)
    assert T % tq == 0
    W = 2 * HEAD_DIM
    q0, k0, v0 = 3 * MOBA_W // W, (3 * MOBA_W + DIFF_W) // W, (3 * MOBA_W + 2 * DIFF_W) // W
    lam_init = 0.8 - 0.6 * math.exp(-0.3 * layer_idx)
    vec = lambda a: a.reshape(1, -1).astype(F32)
    small = lambda n: pl.BlockSpec((1, n), lambda h, i: (0, 0))
    return pl.pallas_call(
        functools.partial(_diff_kernel, tq=tq, scale=HEAD_DIM ** -0.5, lam_init=lam_init),
        grid=(DIFF_HEADS, T // tq),
        in_specs=[small(HEAD_DIM)] * 4 + [small(W),
                  pl.BlockSpec((tq, W), lambda h, i: (i, q0 + h)),
                  pl.BlockSpec((T, W), lambda h, i: (0, k0 + h)),
                  pl.BlockSpec((T, W), lambda h, i: (0, v0 + h))],
        out_specs=pl.BlockSpec((tq, W), lambda h, i: (i, h)),
        out_shape=jax.ShapeDtypeStruct((T, DIFF_W), BF16),
        scratch_shapes=[pltpu.VMEM((2, tq, LANES), F32), pltpu.VMEM((2, tq, LANES), F32),
                        pltpu.VMEM((2, tq, W), F32)],
        compiler_params=_cparams(("parallel", "arbitrary")),
        name="diff_attention",
    )(vec(lq1), vec(lk1), vec(lq2), vec(lk2), vec(g_subln), h0, h0, h0)


def _layernorm(v, g, b):
    mu = jnp.mean(v, axis=1, keepdims=True)
    d = v - mu
    var = jnp.mean(d * d, axis=1, keepdims=True)
    return d * lax.rsqrt(var + LN_EPS) * g + b


def _route(xn, wr_ref, br_ref, idx_ref, gate_ref):
    logits = jnp.dot(xn, wr_ref[...], preferred_element_type=F32,
                     precision=lax.Precision.HIGHEST) + br_ref[...]
    lane = lax.broadcasted_iota(I32, logits.shape, 1)
    lane_f = lane.astype(F32)
    idx_out = jnp.zeros(logits.shape, F32)
    val_out = jnp.full(logits.shape, NEG, F32)
    g = logits
    for r in range(TOP_K):
        mx = jnp.max(g, axis=1, keepdims=True)
        first = jnp.min(jnp.where(g == mx, lane_f, float(LANES)), axis=1, keepdims=True)
        idx_out = jnp.where(lane == r, first, idx_out)
        val_out = jnp.where(lane == r, mx, val_out)
        g = jnp.where(lane_f == first, -jnp.inf, g)
    e = jnp.exp(val_out - jnp.max(val_out, axis=1, keepdims=True))
    gate_ref[...] = e / jnp.sum(e, axis=1, keepdims=True)
    idx_ref[...] = idx_out.astype(I32)


def _mix_ln_kernel(x_ref, mix_ref, g_ref, b_ref, wr_ref, br_ref, xn_ref, xb_ref, idx_ref, gate_ref):
    xn = _layernorm(DN_ALPHA * x_ref[...] + mix_ref[...], g_ref[...], b_ref[...])
    xn_ref[...] = xn
    xb_ref[...] = xn.astype(BF16)
    _route(xn, wr_ref, br_ref, idx_ref, gate_ref)


def _mix_ln_route(x, mix, g, b, w_router, b_router, tm=256):
    T, D = x.shape
    E = w_router.shape[1]
    wr = jnp.pad(w_router.astype(F32), ((0, 0), (0, LANES - E)))
    br = jnp.pad(b_router.astype(F32).reshape(1, E), ((0, 0), (0, LANES - E)), constant_values=NEG)
    row = lambda n: pl.BlockSpec((tm, n), lambda i: (i, 0))
    const = lambda r, n: pl.BlockSpec((r, n), lambda i: (0, 0))
    return pl.pallas_call(
        _mix_ln_kernel,
        grid=(T // tm,),
        in_specs=[row(D), row(D), const(1, D), const(1, D), const(D, LANES), const(1, LANES)],
        out_specs=[row(D), row(D), row(LANES), row(LANES)],
        out_shape=[jax.ShapeDtypeStruct((T, D), F32), jax.ShapeDtypeStruct((T, D), BF16),
                   jax.ShapeDtypeStruct((T, LANES), I32), jax.ShapeDtypeStruct((T, LANES), F32)],
        compiler_params=_cparams(("parallel",)),
        name="mix_ln_route",
    )(x, mix, g.reshape(1, D).astype(F32), b.reshape(1, D).astype(F32), wr, br)


def _combine_ln_kernel(x_ref, y_ref, gate_ref, g_ref, b_ref, xn_ref, xb_ref):
    gates = gate_ref[...]
    f = jnp.zeros(x_ref.shape, F32)
    for r in range(TOP_K):
        f = f + gates[:, r:r + 1] * y_ref[r].astype(F32)
    xn = _layernorm(DN_ALPHA * x_ref[...] + f, g_ref[...], b_ref[...])
    xn_ref[...] = xn
    xb_ref[...] = xn.astype(BF16)


def _combine_ln(x, y4, gates, g, b, tm=256):
    T, D = x.shape
    row = lambda n: pl.BlockSpec((tm, n), lambda i: (i, 0))
    const = lambda r, n: pl.BlockSpec((r, n), lambda i: (0, 0))
    return pl.pallas_call(
        _combine_ln_kernel,
        grid=(T // tm,),
        in_specs=[row(D), pl.BlockSpec((TOP_K, tm, D), lambda i: (0, i, 0)), row(LANES), const(1, D), const(1, D)],
        out_specs=[row(D), row(D)],
        out_shape=[jax.ShapeDtypeStruct((T, D), F32), jax.ShapeDtypeStruct((T, D), BF16)],
        compiler_params=_cparams(("parallel",)),
        name="moe_combine_ln",
    )(x, y4, gates, g.reshape(1, D).astype(F32), b.reshape(1, D).astype(F32))


MOE_TM = 512
MOE_FC = 256


def _expert_kernel(te_ref, na_ref, x_ref, wgu_ref, bgu_ref, wd_ref, bd_ref, sel_ref, y_ref, acc_ref, *, nc):
    i = pl.program_id(0)
    c = pl.program_id(1)

    @pl.when(i < na_ref[0])
    def _():
        @pl.when(c == 0)
        def _():
            acc_ref[...] = jnp.zeros_like(acc_ref)

        hg = _dot(x_ref[...], wgu_ref[...].astype(BF16)) + bgu_ref[...]
        gate = jnp.minimum(hg, SWIGLU_LIMIT)
        up = jnp.clip(hg, -SWIGLU_LIMIT, SWIGLU_LIMIT)
        up_next = pltpu.roll(up, 2 * MOE_FC - 1, axis=1)
        act = (up_next + 1.0) * (gate * (1.0 / (1.0 + jnp.exp(-SWIGLU_ALPHA * gate))))
        act = _dot(act.astype(BF16), sel_ref[...]).astype(BF16)
        acc_ref[...] += _dot(act, wd_ref[...].astype(BF16))

        @pl.when(c == nc - 1)
        def _():
            y_ref[...] = (acc_ref[...] + bd_ref[...]).astype(y_ref.dtype)

    @pl.when((i >= na_ref[0]) & (c == nc - 1))
    def _():
        y_ref[...] = jnp.zeros_like(y_ref)


def _experts(xs, tile_expert, n_active, w_gu, b_gu, w_down, b_down, layer):
    P, D = xs.shape
    _, E, _, F2 = w_gu.shape
    F = F2 // 2
    tm, fc = MOE_TM, min(MOE_FC, F)
    nc = F // fc
    nt = P // tm
    even = (lax.broadcasted_iota(I32, (2 * fc, fc), 0) == 2 * lax.broadcasted_iota(I32, (2 * fc, fc), 1)).astype(BF16)

    def live(i, na):
        return jnp.minimum(i, jnp.maximum(na[0], 1) - 1)

    def chunk(i, c, na):
        return jnp.where(i < na[0], c, nc - 1)

    grid_spec = pltpu.PrefetchScalarGridSpec(
        num_scalar_prefetch=2,
        grid=(nt, nc),
        in_specs=[
            pl.BlockSpec((tm, D), lambda i, c, te, na: (live(i, na), 0)),
            pl.BlockSpec((None, None, D, 2 * fc), lambda i, c, te, na: (layer, te[live(i, na)], 0, chunk(i, c, na))),
            pl.BlockSpec((None, None, 1, 2 * fc), lambda i, c, te, na: (layer, te[live(i, na)], 0, chunk(i, c, na))),
            pl.BlockSpec((None, None, fc, D), lambda i, c, te, na: (layer, te[live(i, na)], chunk(i, c, na), 0)),
            pl.BlockSpec((None, None, 1, D), lambda i, c, te, na: (layer, te[live(i, na)], 0, 0)),
            pl.BlockSpec((2 * fc, fc), lambda i, c, te, na: (0, 0)),
        ],
        out_specs=pl.BlockSpec((tm, D), lambda i, c, te, na: (i, 0)),
        scratch_shapes=[pltpu.VMEM((tm, D), F32)],
    )
    return pl.pallas_call(
        functools.partial(_expert_kernel, nc=nc),
        grid_spec=grid_spec,
        out_shape=jax.ShapeDtypeStruct((P, D), BF16),
        compiler_params=_cparams(("arbitrary", "arbitrary")),
        name="moe_experts",
    )(tile_expert, n_active, xs, w_gu, b_gu.reshape(-1, E, 1, F2), w_down, b_down.reshape(-1, E, 1, D), even)


def _moe(xn, xb, top_idx, gates, w_gu, b_gu, w_down, b_down, layer, ln_g, ln_b):
    T, D = xn.shape
    E = w_gu.shape[1]
    tm = MOE_TM
    e_flat = top_idx[:, :TOP_K].reshape(-1)
    onehot = (e_flat[:, None] == jnp.arange(E, dtype=I32)[None, :]).astype(I32)
    rank = jnp.take_along_axis(jnp.cumsum(onehot, axis=0), e_flat[:, None], axis=1)[:, 0] - 1
    counts = jnp.sum(onehot, axis=0)
    padded = ((counts + tm - 1) // tm) * tm
    ends = jnp.cumsum(padded)
    starts = ends - padded
    slot = starts[e_flat] + rank
    n_tiles = (T * TOP_K + E * (tm - 1)) // tm
    P = n_tiles * tm
    token_of_slot = jnp.zeros((P,), I32).at[slot].set(jnp.arange(T * TOP_K, dtype=I32) // TOP_K)
    tile_expert = jnp.minimum(jnp.searchsorted(ends, jnp.arange(n_tiles, dtype=I32) * tm, side="right"),
                              E - 1).astype(I32)
    n_active = (ends[-1:] // tm).astype(I32)

    xs = jnp.take(xb, token_of_slot, axis=0, mode="clip")
    y = _experts(xs, tile_expert, n_active, w_gu, b_gu, w_down, b_down, layer)
    slot_km = slot.reshape(T, TOP_K).T.reshape(-1)
    y4 = jnp.take(y, slot_km, axis=0, mode="clip").reshape(TOP_K, T, D)
    return _combine_ln(xn, y4, gates, ln_g, ln_b)


L1_IN_PAD = 1792


def _l1_post_kernel(h_ref, gq_ref, gkv_ref, gk_ref, bk_ref, c_ref, s_ref, cq_ref, kc_ref, ki_ref, wi_ref):
    def rms(v, g):
        return v * lax.rsqrt(jnp.mean(v * v, axis=1, keepdims=True) + 1e-6) * g

    cq_ref[...] = rms(h_ref[:, :Q_LORA], gq_ref[...]).astype(BF16)
    kc_ref[:, :KV_LORA] = rms(h_ref[:, Q_LORA:Q_LORA + KV_LORA], gkv_ref[...]).astype(BF16)
    c, s = c_ref[...], s_ref[...]
    o_idx = Q_LORA + KV_LORA
    ki = _layernorm(h_ref[:, o_idx:o_idx + IDX_DIM], gk_ref[...], bk_ref[...])
    ki_ref[...] = _rope128(ki, c, s, _swap_pair).astype(BF16)
    tail = h_ref[:, o_idx + IDX_DIM:]
    kc_ref[:, KV_LORA:] = _rope128(tail, c, s, _swap_pair)[:, :QK_ROPE].astype(BF16)
    wi_ref[...] = tail * (IDX_HEADS ** -0.5 * IDX_DIM ** -0.5)


def _l1_post(h1, g_q, g_kv, g_kidx, b_kidx, idx_tab, tm=256):
    T = h1.shape[0]
    row = lambda n: pl.BlockSpec((tm, n), lambda i: (i, 0))
    const = lambda n: pl.BlockSpec((1, n), lambda i: (0, 0))
    vec = lambda a: a.reshape(1, -1).astype(F32)
    return pl.pallas_call(
        _l1_post_kernel,
        grid=(T // tm,),
        in_specs=[row(L1_IN_PAD), const(Q_LORA), const(KV_LORA), const(IDX_DIM), const(IDX_DIM),
                  row(LANES), row(LANES)],
        out_specs=[row(Q_LORA), row(KV_LORA + QK_ROPE), row(IDX_DIM), row(LANES)],
        out_shape=[jax.ShapeDtypeStruct((T, Q_LORA), BF16), jax.ShapeDtypeStruct((T, KV_LORA + QK_ROPE), BF16),
                   jax.ShapeDtypeStruct((T, IDX_DIM), BF16), jax.ShapeDtypeStruct((T, LANES), F32)],
        compiler_params=_cparams(("parallel",)),
        name="l1_post",
    )(h1, vec(g_q), vec(g_kv), vec(g_kidx), vec(b_kidx), idx_tab[0], idx_tab[1])


def _qlat_kernel(qn_ref, wuk_ref, qpe_ref, o_ref):
    ql = _dot_nt(qn_ref[...], wuk_ref[...].astype(BF16))
    o_ref[:, :KV_LORA] = ql.astype(BF16)
    o_ref[:, KV_LORA:] = qpe_ref[...]


def _q_absorb(q_nope, w_uk, q_pe_h, tm=1024):
    T = q_nope.shape[0]
    tm = min(tm, T)
    H = MLA_HEADS
    return pl.pallas_call(
        _qlat_kernel,
        grid=(H, T // tm),
        in_specs=[pl.BlockSpec((tm, QK_NOPE), lambda h, i: (i, h)),
                  pl.BlockSpec((KV_LORA, QK_NOPE), lambda h, i: (0, h)),
                  pl.BlockSpec((None, tm, QK_ROPE), lambda h, i: (h, i, 0))],
        out_specs=pl.BlockSpec((None, tm, KV_LORA + QK_ROPE), lambda h, i: (h, i, 0)),
        out_shape=jax.ShapeDtypeStruct((H, T, KV_LORA + QK_ROPE), BF16),
        compiler_params=_cparams(("parallel", "parallel")),
        name="q_absorb",
    )(q_nope, w_uk, q_pe_h)


DSA_TQ = 64
DSA_KB = 256
DSA_TK = 512
DSA_HG = 8


def _indexer_kernel(qi_ref, w_ref, ki_ref, bias_ref, key_ref, *, T, tq, n_sel):
    i = pl.program_id(0)
    kb = DSA_KB
    nkb = T // kb
    n_live = ((i + 1) * tq + kb - 1) // kb
    q = qi_ref[...].reshape(IDX_HEADS * tq, IDX_DIM)
    w = w_ref[...]

    def score_block(j, carry):
        rows = pl.ds(pl.multiple_of(j * kb, kb), kb)
        s = _dot_nt(q, ki_ref[rows, :])
        isc = jnp.sum(jnp.maximum(s, 0.0).reshape(IDX_HEADS, tq, kb) * w, axis=0) + 0.0
        qpos = i * tq + lax.broadcasted_iota(I32, (tq, kb), 0)
        kpos = j * kb + lax.broadcasted_iota(I32, (tq, kb), 1)
        isc = jnp.where(kpos <= qpos, isc, -jnp.inf)
        bits = pltpu.bitcast(isc, I32)
        key_ref[j] = jnp.where(bits < 0, bits ^ 0x7FFFFFFF, bits)
        return carry

    lax.fori_loop(0, n_live, score_block, 0)

    def count_ge(cand):
        cb = jnp.broadcast_to(cand, (tq, LANES))

        def body(j, c):
            blk = key_ref[j]
            for g in range(kb // LANES):
                c = c + (blk[:, g * LANES:(g + 1) * LANES] >= cb).astype(F32)
            return c

        c = lax.fori_loop(0, n_live, body, jnp.zeros((tq, LANES), F32))
        return jnp.sum(c, axis=1, keepdims=True)

    thr = jnp.full((tq, 1), -2**31, I32)
    for bit in range(31, -1, -1):
        cand = jnp.zeros((tq, 1), I32) if bit == 31 else thr + (1 << bit)
        thr = jnp.where(count_ge(cand) >= n_sel, cand, thr)

    def write_block(j, carry):
        blk = key_ref[j]
        bias_ref[j] = jnp.where((blk >= thr) & (blk > KEY_NEG_INF), 0.0, NEG).astype(BF16)
        return carry

    lax.fori_loop(0, n_live, write_block, 0)

    def dead_block(j, carry):
        bias_ref[j] = jnp.full((tq, kb), NEG, BF16)
        return carry

    lax.fori_loop(n_live, nkb, dead_block, 0)


def _indexer_mask(qi_h, w_col, ki, T, n_sel):
    tq, kb = min(DSA_TQ, T), DSA_KB
    nkb = T // kb
    return pl.pallas_call(
        functools.partial(_indexer_kernel, T=T, tq=tq, n_sel=n_sel),
        grid=(T // tq,),
        in_specs=[pl.BlockSpec((IDX_HEADS, tq, IDX_DIM), lambda i: (0, i, 0)),
                  pl.BlockSpec((IDX_HEADS, tq, 1), lambda i: (0, i, 0)),
                  pl.BlockSpec((T, IDX_DIM), lambda i: (0, 0), pipeline_mode=pl.Buffered(1))],
        out_specs=pl.BlockSpec((None, nkb, tq, kb), lambda i: (i, 0, 0, 0)),
        out_shape=jax.ShapeDtypeStruct((T // tq, nkb, tq, kb), BF16),
        scratch_shapes=[pltpu.VMEM((nkb, tq, kb), I32)],
        compiler_params=_cparams(("parallel",)),
        name="dsa_indexer",
    )(qi_h, w_col, ki)


def _dsa_kernel(q_ref, bias_ref, kc_ref, wuv_ref, o_ref, s_ref, p_ref, a_ref, m_ref, l_ref, acc_ref, *, tq, scale):
    i = pl.program_id(0)
    H, tk = MLA_HEADS, DSA_TK
    R = H * tq
    per = tk // DSA_KB
    n_steps = ((i + 1) * tq + tk - 1) // tk
    c = scale * LOG2E
    q = q_ref[...].reshape(R, KV_LORA + QK_ROPE)

    def keys(j):
        return kc_ref[pl.ds(pl.multiple_of(j * tk, tk), tk), :]

    m_ref[...] = jnp.full_like(m_ref, NEG)
    l_ref[...] = jnp.zeros_like(l_ref)
    acc_ref[...] = jnp.zeros_like(acc_ref)
    p_ref[1] = jnp.zeros((R, tk), BF16)
    a_ref[1] = jnp.ones((R, LANES), F32)
    s_ref[0] = _dot_nt(q, keys(0))
    lane_tiles = [slice(t * LANES, (t + 1) * LANES) for t in range(tk // LANES)]

    def add_values(slot, j):
        pv = _dot(p_ref[slot], keys(j)[:, :KV_LORA])
        a = a_ref[slot]
        for t in range(KV_LORA // LANES):
            cols = slice(t * LANES, (t + 1) * LANES)
            acc_ref[:, cols] = a * acc_ref[:, cols] + pv[:, cols]

    def step(j, carry):
        cur = j % 2
        s_ref[1 - cur] = _dot_nt(q, keys(jnp.minimum(j + 1, n_steps - 1)))
        add_values(1 - cur, jnp.maximum(j - 1, 0))
        bias = jnp.concatenate([bias_ref[j * per + b] for b in range(per)], axis=1).astype(F32)
        for h in range(H):
            r = slice(h * tq, (h + 1) * tq)
            s = s_ref[cur, r, :] + bias
            m_prev = m_ref[r, :]
            m_new = jnp.maximum(m_prev, jnp.max(s, axis=1, keepdims=True))
            alpha = jnp.exp2((m_prev - m_new) * c)
            p = [jnp.exp2((s[:, t] - m_new) * c) for t in lane_tiles]
            l_ref[r, :] = alpha * l_ref[r, :] + sum(p[1:], p[0])
            m_ref[r, :] = m_new
            a_ref[cur, r, :] = alpha
            for t, pt in zip(lane_tiles, p):
                p_ref[cur, r, t] = pt.astype(BF16)
        return carry

    lax.fori_loop(0, n_steps, step, 0)
    add_values((n_steps - 1) % 2, n_steps - 1)

    for h in range(H):
        r = slice(h * tq, (h + 1) * tq)
        l = jnp.sum(l_ref[r, :], axis=1, keepdims=True)
        o_lat = (acc_ref[r, :] / l).astype(BF16)
        o_ref[:, h * V_HEAD:(h + 1) * V_HEAD] = _dot(o_lat, wuv_ref[h]).astype(o_ref.dtype)


def _dsa_attention(q576, bias, kc, w_uv_h, T):
    tq = min(DSA_TQ, T)
    H = MLA_HEADS
    C = KV_LORA + QK_ROPE
    nkb = T // DSA_KB
    assert T % DSA_TK == 0
    return pl.pallas_call(
        functools.partial(_dsa_kernel, tq=tq, scale=(QK_NOPE + QK_ROPE) ** -0.5),
        grid=(T // tq,),
        in_specs=[pl.BlockSpec((H, tq, C), lambda i: (0, i, 0)),
                  pl.BlockSpec((None, nkb, tq, DSA_KB), lambda i: (i, 0, 0, 0)),
                  pl.BlockSpec((T, C), lambda i: (0, 0), pipeline_mode=pl.Buffered(1)),
                  pl.BlockSpec((H, KV_LORA, V_HEAD), lambda i: (0, 0, 0), pipeline_mode=pl.Buffered(1))],
        out_specs=pl.BlockSpec((tq, H * V_HEAD), lambda i: (i, 0)),
        out_shape=jax.ShapeDtypeStruct((T, H * V_HEAD), BF16),
        scratch_shapes=[pltpu.VMEM((2, H * tq, DSA_TK), F32), pltpu.VMEM((2, H * tq, DSA_TK), BF16),
                        pltpu.VMEM((2, H * tq, LANES), F32),
                        pltpu.VMEM((H * tq, LANES), F32), pltpu.VMEM((H * tq, LANES), F32),
                        pltpu.VMEM((H * tq, KV_LORA), F32)],
        compiler_params=_cparams(("parallel",)),
        name="dsa_attention",
    )(q576, bias, kc, w_uv_h)


def _even_mixer(xb, T, w_in, w_out, lq1, lk1, lq2, lk2, g_subln, layer_idx, tabs):
    full = tabs[0]
    rope = [(0, 2 * MOBA_W, full, _swap_full),
            (3 * MOBA_W, 3 * MOBA_W + 2 * DIFF_W, full, _swap_full)]
    h0 = _matmul(xb, w_in, tm=2048, tn=1024, tk=512, out_dtype=BF16, rope=rope, name="l0_in_proj")
    o_moba = _moba_attention(h0, T)
    o_diff = _diff_attention(h0, T, lq1, lk1, lq2, lk2, g_subln, layer_idx)
    o = jnp.concatenate([o_moba, o_diff], axis=1)
    return _matmul(o, w_out, tm=2048, tn=1024, tk=512, out_dtype=F32, name="l0_out_proj")


def _odd_mixer(xb, T, w_in, g_q, g_kv, w_qb, w_uk, w_uv, w_iq, g_kidx, b_kidx, w_out, tabs):
    _, pair, idx = tabs
    D = w_in.shape[0]
    H = MLA_HEADS
    o_kpe, o_kidx, o_w = Q_LORA + KV_LORA, Q_LORA + KV_LORA + QK_ROPE, Q_LORA + KV_LORA + QK_ROPE + IDX_DIM
    w_in_p = jnp.concatenate([w_in[:, :o_kpe], w_in[:, o_kidx:o_w], w_in[:, o_kpe:o_kidx], w_in[:, o_w:],
                              jnp.zeros((D, L1_IN_PAD - w_in.shape[1]), w_in.dtype)], axis=1)
    h1 = _matmul(xb, w_in_p, tm=2048, tn=L1_IN_PAD // 2, tk=512, out_dtype=F32, name="l1_in_proj")
    cq, kc, ki, wi_full = _l1_post(h1, g_q, g_kv, g_kidx, b_kidx, idx)
    w_qb3 = w_qb.reshape(Q_LORA, H, QK_NOPE + QK_ROPE)
    w_qb_p = jnp.concatenate([w_qb3[:, :, :QK_NOPE].reshape(Q_LORA, H * QK_NOPE),
                              w_qb3[:, :, QK_NOPE:].reshape(Q_LORA, H * QK_ROPE)], axis=1)
    n0 = H * QK_NOPE
    q = _matmul(cq, w_qb_p, tm=2048, tn=1024, tk=Q_LORA, out_dtype=BF16,
                rope=[(n0, n0 + H * QK_ROPE, pair, _swap_pair)], name="l1_q_proj")
    qi = _matmul(cq, w_iq, tm=2048, tn=1024, tk=Q_LORA, out_dtype=BF16,
                 rope=[(0, IDX_HEADS * IDX_DIM, idx, _swap_pair)], name="l1_qi_proj")
    q_pe_h = q[:, n0:].reshape(T, H, QK_ROPE).transpose(1, 0, 2)
    q576 = _q_absorb(q, w_uk, q_pe_h)
    qi_h = qi.reshape(T, IDX_HEADS, IDX_DIM).transpose(1, 0, 2)
    w_col = wi_full[:, QK_ROPE:QK_ROPE + IDX_HEADS].T[:, :, None]
    bias = _indexer_mask(qi_h, w_col, ki, T, min(IDX_TOPK_MAX, T // 4))
    w_uv_h = w_uv.reshape(KV_LORA, H, V_HEAD).transpose(1, 0, 2).astype(BF16)
    o = _dsa_attention(q576, bias, kc, w_uv_h, T)
    return _matmul(o, w_out, tm=2048, tn=1024, tk=512, out_dtype=F32, name="l1_out_proj")


def kernel(x, l0_w_in, l0_w_out, l0_lam_q1, l0_lam_k1, l0_lam_q2, l0_lam_k2, l0_g_subln, l1_w_in, l1_g_q, l1_g_kv, l1_w_qb, l1_w_uk, l1_w_uv, l1_w_iq, l1_g_kidx, l1_b_kidx, l1_w_out, ln_g, ln_b, moe_w_router, moe_b_router, moe_w_gu, moe_b_gu, moe_w_down, moe_b_down):
    B, T, D = x.shape
    assert B == 1
    xn = x.reshape(T, D)
    xb = xn.astype(BF16)
    tabs = _rope_tables(T)
    for i in range(DEPTH):
        if i % 2 == 0:
            mix = _even_mixer(xb, T, l0_w_in, l0_w_out, l0_lam_q1, l0_lam_k1, l0_lam_q2, l0_lam_k2,
                              l0_g_subln, i, tabs)
        else:
            mix = _odd_mixer(xb, T, l1_w_in, l1_g_q, l1_g_kv, l1_w_qb, l1_w_uk, l1_w_uv, l1_w_iq,
                             l1_g_kidx, l1_b_kidx, l1_w_out, tabs)
        xn, xb, top_idx, gates = _mix_ln_route(xn, mix, ln_g[i, 0], ln_b[i, 0], moe_w_router[i], moe_b_router[i])
        xn, xb = _moe(xn, xb, top_idx, gates, moe_w_gu, moe_b_gu, moe_w_down, moe_b_down, i,
                      ln_g[i, 1], ln_b[i, 1])
    return xn.reshape(B, T, D)
```

```python
import functools
import math

import jax
import jax.numpy as jnp
from jax import lax
from jax.experimental import pallas as pl
from jax.experimental.pallas import tpu as pltpu

F32 = jnp.float32
BF16 = jnp.bfloat16
I32 = jnp.int32

HEAD_DIM = 128
ROPE_THETA = 10000.0
MOBA_HEADS = 16
MOBA_BLOCK = 256
MOBA_TOPK = 3
DIFF_HEADS = 8
MOBA_W = MOBA_HEADS * HEAD_DIM
DIFF_W = DIFF_HEADS * 2 * HEAD_DIM
MLA_HEADS = 32
Q_LORA = 1024
KV_LORA = 512
QK_NOPE = 128
QK_ROPE = 64
V_HEAD = 128
IDX_HEADS = 32
IDX_DIM = 128
IDX_TOPK_MAX = 256
N_EXPERTS = 32
TOP_K = 4
SWIGLU_LIMIT = 7.0
SWIGLU_ALPHA = 1.702
DEPTH = 2
DN_ALPHA = (2 * DEPTH) ** 0.25
LN_EPS = 1e-5

LANES = 128
VMEM_LIMIT_BYTES = 56 * 2**20

NEG = -1e30
LOG2E = 1.4426950408889634
KEY_NEG_INF = -2139095041


def _cparams(semantics, flags=None):
    return pltpu.CompilerParams(dimension_semantics=semantics, vmem_limit_bytes=VMEM_LIMIT_BYTES, flags=flags)


def _dot(a, b):
    return jnp.dot(a, b, preferred_element_type=F32)


def _dot_nt(a, b, precision=None):
    return lax.dot_general(a, b, (((1,), (1,)), ((), ())), preferred_element_type=F32, precision=precision)


def _rope_tables(T):
    pos = jnp.arange(T).astype(F32)[:, None]
    inv64 = ROPE_THETA ** (-jnp.arange(64, dtype=F32) / 64)
    a64 = pos * inv64[None, :]
    c64, s64 = jnp.cos(a64), jnp.sin(a64)
    inv32 = ROPE_THETA ** (-jnp.arange(32, dtype=F32) / 32)
    a32 = pos * inv32[None, :]
    c32, s32 = jnp.cos(a32), jnp.sin(a32)
    one, zero = jnp.ones_like(c32), jnp.zeros_like(c32)
    full = (jnp.concatenate([c64, c64], 1), jnp.concatenate([-s64, s64], 1))
    pair = (jnp.concatenate([c32, c32, c32, c32], 1), jnp.concatenate([-s32, s32, -s32, s32], 1))
    idx = (jnp.concatenate([c32, c32, one, one], 1), jnp.concatenate([-s32, s32, zero, zero], 1))
    return full, pair, idx


def _swap_full(z):
    return pltpu.roll(z, 64, axis=1)


def _swap_pair(z):
    lane = lax.broadcasted_iota(I32, z.shape, 1)
    return jnp.where((lane % 64) < 32, pltpu.roll(z, 96, axis=1), pltpu.roll(z, 32, axis=1))


def _rope128(z, c, s, swap):
    return z * c + swap(z) * s


def _mm_kernel(*refs, nk, tn, rope_ranges, swaps):
    n_tab = len(swaps)
    a_ref, b_ref = refs[0], refs[1]
    tab_refs = refs[2:2 + 2 * n_tab]
    o_ref, acc_ref = refs[2 + 2 * n_tab], refs[3 + 2 * n_tab]
    k = pl.program_id(2)

    @pl.when(k == 0)
    def _():
        acc_ref[...] = jnp.zeros_like(acc_ref)

    acc_ref[...] += _dot(a_ref[...].astype(BF16), b_ref[...].astype(BF16))

    @pl.when(k == nk - 1)
    def _():
        if not rope_ranges:
            o_ref[...] = acc_ref[...].astype(o_ref.dtype)
            return
        j = pl.program_id(1)
        plain = None
        for (lo, hi, t) in rope_ranges:
            hit = (j >= lo) & (j < hi)
            plain = hit if plain is None else (plain | hit)

            @pl.when(hit)
            def _(t=t):
                c = tab_refs[2 * t][...]
                s = tab_refs[2 * t + 1][...]
                for g in range(tn // LANES):
                    z = acc_ref[:, g * LANES:(g + 1) * LANES]
                    o_ref[:, g * LANES:(g + 1) * LANES] = _rope128(z, c, s, swaps[t]).astype(o_ref.dtype)

        @pl.when(jnp.logical_not(plain))
        def _():
            o_ref[...] = acc_ref[...].astype(o_ref.dtype)


def _matmul(a, b, *, tm, tn, tk, out_dtype, rope=None, name="mm"):
    M, K = a.shape
    _, N = b.shape
    tm, tn, tk = min(tm, M), min(tn, N), min(tk, K)
    assert M % tm == 0 and N % tn == 0 and K % tk == 0
    rope = rope or []
    ranges, tabs, swaps = [], [], []
    for t, (lo, hi, (c, s), swap) in enumerate(rope):
        assert lo % tn == 0 and hi % tn == 0 and tn % LANES == 0
        ranges.append((lo // tn, hi // tn, t))
        tabs += [c, s]
        swaps.append(swap)
    nk = K // tk
    in_specs = [pl.BlockSpec((tm, tk), lambda i, j, k: (i, k)),
                pl.BlockSpec((tk, tn), lambda i, j, k: (k, j))]
    in_specs += [pl.BlockSpec((tm, LANES), lambda i, j, k: (i, 0)) for _ in tabs]
    return pl.pallas_call(
        functools.partial(_mm_kernel, nk=nk, tn=tn, rope_ranges=tuple(ranges), swaps=tuple(swaps)),
        grid=(M // tm, N // tn, nk),
        in_specs=in_specs,
        out_specs=pl.BlockSpec((tm, tn), lambda i, j, k: (i, j)),
        out_shape=jax.ShapeDtypeStruct((M, N), out_dtype),
        scratch_shapes=[pltpu.VMEM((tm, tn), F32)],
        compiler_params=_cparams(("parallel", "parallel", "arbitrary")),
        name=name,
    )(a, b, *tabs)


SOFTMAX_ROWS = 64


def _softmax_step(s, v, m_ref, l_ref, acc_ref, scale):
    c = scale * LOG2E
    tq, tk = s.shape
    dv = v.shape[1]
    rows = [slice(r, r + SOFTMAX_ROWS) for r in range(0, tq, SOFTMAX_ROWS)]
    p_rows, alphas = [], []
    for r in rows:
        sc = s[r, :]
        m_prev = m_ref[r, :]
        m_new = jnp.maximum(m_prev, jnp.max(sc, axis=1, keepdims=True))
        alpha = jnp.exp2((m_prev - m_new) * c)
        p = [jnp.exp2((sc[:, t:t + LANES] - m_new) * c) for t in range(0, tk, LANES)]
        l_ref[r, :] = alpha * l_ref[r, :] + sum(p[1:], p[0])
        m_ref[r, :] = m_new
        p_rows.append(jnp.concatenate([pt.astype(BF16) for pt in p], axis=1))
        alphas.append(alpha)
    pv = _dot(jnp.concatenate(p_rows, axis=0), v)
    for r, alpha in zip(rows, alphas):
        for t in range(0, dv, LANES):
            acc_ref[r, t:t + LANES] = alpha * acc_ref[r, t:t + LANES] + pv[r, t:t + LANES]


def _row_sum(l_ref):
    return jnp.sum(l_ref[...], axis=1, keepdims=True)


def _causal_mask(s, row0, col0):
    row = row0 + lax.broadcasted_iota(I32, s.shape, 0)
    col = col0 + lax.broadcasted_iota(I32, s.shape, 1)
    return jnp.where(col <= row, s, NEG)


MOBA_TILE = 512


def _moba_kernel(q_ref, k_ref, v_ref, o_ref, kmean_ref, m_ref, l_ref, acc_ref, *, T, tq, scale):
    i = pl.program_id(1)
    nb = T // MOBA_BLOCK
    per = tq // MOBA_BLOCK

    @pl.when(i == 0)
    def _():
        blk = lax.broadcasted_iota(I32, (LANES, T), 0)
        pos = lax.broadcasted_iota(I32, (LANES, T), 1)
        avg = jnp.where(pos // MOBA_BLOCK == blk, 1.0 / MOBA_BLOCK, 0.0).astype(BF16)
        kmean_ref[...] = _dot(avg, k_ref[...])

    q = q_ref[...]
    gate = _dot_nt(q.astype(F32), kmean_ref[...], precision=lax.Precision.HIGHEST)
    lane = lax.broadcasted_iota(I32, (tq, LANES), 1)
    lane_f = lane.astype(F32)
    own = i * per + lax.broadcasted_iota(I32, (tq, LANES), 0) // MOBA_BLOCK
    past = lane < own
    g = jnp.where(past, gate, -jnp.inf)
    sel = jnp.zeros((tq, LANES), jnp.bool_)
    for _ in range(min(MOBA_TOPK, nb)):
        mx = jnp.max(g, axis=1, keepdims=True)
        first = jnp.min(jnp.where(g == mx, lane_f, float(LANES)), axis=1, keepdims=True)
        pick = lane_f == first
        sel = sel | pick
        g = jnp.where(pick, -jnp.inf, g)
    visible = (sel & past) | (lane == own)
    q_aug = jnp.concatenate([q, jnp.where(visible, 0.0, NEG).astype(BF16)], axis=1)

    m_ref[...] = jnp.full_like(m_ref, NEG)
    l_ref[...] = jnp.zeros_like(l_ref)
    acc_ref[...] = jnp.zeros_like(acc_ref)
    key_blk = lax.broadcasted_iota(I32, (tq, LANES), 0) // MOBA_BLOCK
    key_lane = lax.broadcasted_iota(I32, (tq, LANES), 1)

    def tile(j, causal):
        rows = pl.ds(pl.multiple_of(j * tq, tq), tq)
        k_aug = jnp.concatenate([k_ref[rows, :], (key_lane == j * per + key_blk).astype(BF16)], axis=1)
        s = _dot_nt(q_aug, k_aug)
        if causal:
            s = _causal_mask(s, 0, 0)
        _softmax_step(s, v_ref[rows, :], m_ref, l_ref, acc_ref, scale)

    def past_tile(j, carry):
        tile(j, False)
        return carry

    lax.fori_loop(0, i, past_tile, 0)
    tile(i, True)
    o_ref[...] = (acc_ref[...] / _row_sum(l_ref)).astype(o_ref.dtype)


def _moba_attention(h0, T):
    tq = min(MOBA_TILE, T)
    assert T % tq == 0 and tq % MOBA_BLOCK == 0 and T // MOBA_BLOCK <= LANES
    H = MOBA_HEADS
    return pl.pallas_call(
        functools.partial(_moba_kernel, T=T, tq=tq, scale=HEAD_DIM ** -0.5),
        grid=(H, T // tq),
        in_specs=[pl.BlockSpec((tq, HEAD_DIM), lambda h, i: (i, h)),
                  pl.BlockSpec((T, HEAD_DIM), lambda h, i: (0, H + h)),
                  pl.BlockSpec((T, HEAD_DIM), lambda h, i: (0, 2 * H + h))],
        out_specs=pl.BlockSpec((tq, HEAD_DIM), lambda h, i: (i, h)),
        out_shape=jax.ShapeDtypeStruct((T, MOBA_W), BF16),
        scratch_shapes=[pltpu.VMEM((LANES, HEAD_DIM), F32),
                        pltpu.VMEM((tq, LANES), F32), pltpu.VMEM((tq, LANES), F32),
                        pltpu.VMEM((tq, HEAD_DIM), F32)],
        compiler_params=_cparams(("parallel", "arbitrary")),
        name="moba_attention",
    )(h0, h0, h0)


DIFF_TQ = 512


def _diff_kernel(lq1_ref, lk1_ref, lq2_ref, lk2_ref, g_ref, q_ref, k_ref, v_ref, o_ref,
                 m_ref, l_ref, acc_ref, *, tq, scale, lam_init):
    i = pl.program_id(1)
    m_ref[...] = jnp.full_like(m_ref, NEG)
    l_ref[...] = jnp.zeros_like(l_ref)
    acc_ref[...] = jnp.zeros_like(acc_ref)
    q = q_ref[...]

    def block(j, masked):
        rows = pl.ds(pl.multiple_of(j * tq, tq), tq)
        kj = k_ref[rows, :]
        vj = v_ref[rows, :]
        cols = [slice(mp * HEAD_DIM, (mp + 1) * HEAD_DIM) for mp in range(2)]
        scores = [_dot_nt(q[:, c], kj[:, c]) for c in cols]
        for mp in range(2):
            s = _causal_mask(scores[mp], 0, 0) if masked else scores[mp]
            _softmax_step(s, vj, m_ref.at[mp], l_ref.at[mp], acc_ref.at[mp], scale)

    def past_block(j, carry):
        block(j, False)
        return carry

    lax.fori_loop(0, i, past_block, 0)
    block(i, True)

    lam = (jnp.exp(jnp.sum(lq1_ref[...] * lk1_ref[...], axis=1, keepdims=True))
           - jnp.exp(jnp.sum(lq2_ref[...] * lk2_ref[...], axis=1, keepdims=True)) + lam_init)
    o = acc_ref[0] / _row_sum(l_ref.at[0]) - lam * (acc_ref[1] / _row_sum(l_ref.at[1]))
    o = o * lax.rsqrt(jnp.mean(o * o, axis=1, keepdims=True) + 1e-5) * g_ref[...]
    o_ref[...] = (o * (1.0 - lam_init)).astype(o_ref.dtype)


def _diff_attention(h0, T, lq1, lk1, lq2, lk2, g_subln, layer_idx):
    tq = min(DIFF_TQ, T)
    assert T % tq == 0
    W = 2 * HEAD_DIM
    q0, k0, v0 = 3 * MOBA_W // W, (3 * MOBA_W + DIFF_W) // W, (3 * MOBA_W + 2 * DIFF_W) // W
    lam_init = 0.8 - 0.6 * math.exp(-0.3 * layer_idx)
    vec = lambda a: a.reshape(1, -1).astype(F32)
    small = lambda n: pl.BlockSpec((1, n), lambda h, i: (0, 0))
    return pl.pallas_call(
        functools.partial(_diff_kernel, tq=tq, scale=HEAD_DIM ** -0.5, lam_init=lam_init),
        grid=(DIFF_HEADS, T // tq),
        in_specs=[small(HEAD_DIM)] * 4 + [small(W),
                  pl.BlockSpec((tq, W), lambda h, i: (i, q0 + h)),
                  pl.BlockSpec((T, W), lambda h, i: (0, k0 + h)),
                  pl.BlockSpec((T, W), lambda h, i: (0, v0 + h))],
        out_specs=pl.BlockSpec((tq, W), lambda h, i: (i, h)),
        out_shape=jax.ShapeDtypeStruct((T, DIFF_W), BF16),
        scratch_shapes=[pltpu.VMEM((2, tq, LANES), F32), pltpu.VMEM((2, tq, LANES), F32),
                        pltpu.VMEM((2, tq, W), F32)],
        compiler_params=_cparams(("parallel", "arbitrary")),
        name="diff_attention",
    )(vec(lq1), vec(lk1), vec(lq2), vec(lk2), vec(g_subln), h0, h0, h0)


def _layernorm(v, g, b):
    mu = jnp.mean(v, axis=1, keepdims=True)
    d = v - mu
    var = jnp.mean(d * d, axis=1, keepdims=True)
    return d * lax.rsqrt(var + LN_EPS) * g + b


def _route(xn, wr_ref, br_ref, idx_ref, gate_ref):
    logits = jnp.dot(xn, wr_ref[...], preferred_element_type=F32,
                     precision=lax.Precision.HIGHEST) + br_ref[...]
    lane = lax.broadcasted_iota(I32, logits.shape, 1)
    lane_f = lane.astype(F32)
    idx_out = jnp.zeros(logits.shape, F32)
    val_out = jnp.full(logits.shape, NEG, F32)
    g = logits
    for r in range(TOP_K):
        mx = jnp.max(g, axis=1, keepdims=True)
        first = jnp.min(jnp.where(g == mx, lane_f, float(LANES)), axis=1, keepdims=True)
        idx_out = jnp.where(lane == r, first, idx_out)
        val_out = jnp.where(lane == r, mx, val_out)
        g = jnp.where(lane_f == first, -jnp.inf, g)
    e = jnp.exp(val_out - jnp.max(val_out, axis=1, keepdims=True))
    gate_ref[...] = e / jnp.sum(e, axis=1, keepdims=True)
    idx_ref[...] = idx_out.astype(I32)


def _mix_ln_kernel(x_ref, mix_ref, g_ref, b_ref, wr_ref, br_ref, xn_ref, xb_ref, idx_ref, gate_ref):
    xn = _layernorm(DN_ALPHA * x_ref[...] + mix_ref[...], g_ref[...], b_ref[...])
    xn_ref[...] = xn
    xb_ref[...] = xn.astype(BF16)
    _route(xn, wr_ref, br_ref, idx_ref, gate_ref)


def _mix_ln_route(x, mix, g, b, w_router, b_router, tm=256):
    T, D = x.shape
    E = w_router.shape[1]
    wr = jnp.pad(w_router.astype(F32), ((0, 0), (0, LANES - E)))
    br = jnp.pad(b_router.astype(F32).reshape(1, E), ((0, 0), (0, LANES - E)), constant_values=NEG)
    row = lambda n: pl.BlockSpec((tm, n), lambda i: (i, 0))
    const = lambda r, n: pl.BlockSpec((r, n), lambda i: (0, 0))
    return pl.pallas_call(
        _mix_ln_kernel,
        grid=(T // tm,),
        in_specs=[row(D), row(D), const(1, D), const(1, D), const(D, LANES), const(1, LANES)],
        out_specs=[row(D), row(D), row(LANES), row(LANES)],
        out_shape=[jax.ShapeDtypeStruct((T, D), F32), jax.ShapeDtypeStruct((T, D), BF16),
                   jax.ShapeDtypeStruct((T, LANES), I32), jax.ShapeDtypeStruct((T, LANES), F32)],
        compiler_params=_cparams(("parallel",)),
        name="mix_ln_route",
    )(x, mix, g.reshape(1, D).astype(F32), b.reshape(1, D).astype(F32), wr, br)


def _combine_ln_kernel(x_ref, y_ref, gate_ref, g_ref, b_ref, xn_ref, xb_ref):
    gates = gate_ref[...]
    f = jnp.zeros(x_ref.shape, F32)
    for r in range(TOP_K):
        f = f + gates[:, r:r + 1] * y_ref[r].astype(F32)
    xn = _layernorm(DN_ALPHA * x_ref[...] + f, g_ref[...], b_ref[...])
    xn_ref[...] = xn
    xb_ref[...] = xn.astype(BF16)


def _combine_ln(x, y4, gates, g, b, tm=256):
    T, D = x.shape
    row = lambda n: pl.BlockSpec((tm, n), lambda i: (i, 0))
    const = lambda r, n: pl.BlockSpec((r, n), lambda i: (0, 0))
    return pl.pallas_call(
        _combine_ln_kernel,
        grid=(T // tm,),
        in_specs=[row(D), pl.BlockSpec((TOP_K, tm, D), lambda i: (0, i, 0)), row(LANES), const(1, D), const(1, D)],
        out_specs=[row(D), row(D)],
        out_shape=[jax.ShapeDtypeStruct((T, D), F32), jax.ShapeDtypeStruct((T, D), BF16)],
        compiler_params=_cparams(("parallel",)),
        name="moe_combine_ln",
    )(x, y4, gates, g.reshape(1, D).astype(F32), b.reshape(1, D).astype(F32))


MOE_TM = 1152
MOE_FC = 256
MOE_DC = 1024


def _expert_kernel(te_ref, na_ref, x_ref, wgu_ref, bgu_ref, wd_ref, bd_ref, sel_ref, y_ref, act_ref, *, ng, fc):
    i = pl.program_id(0)
    s = pl.program_id(1)
    live = i < na_ref[0]

    @pl.when(live & (s < ng))
    def _():
        hg = _dot(x_ref[...], wgu_ref[...].astype(BF16)) + bgu_ref[...]
        gate = jnp.minimum(hg, SWIGLU_LIMIT)
        up = jnp.clip(hg, -SWIGLU_LIMIT, SWIGLU_LIMIT)
        up_next = pltpu.roll(up, 2 * fc - 1, axis=1)
        act = (up_next + 1.0) * (gate * (1.0 / (1.0 + jnp.exp(-SWIGLU_ALPHA * gate))))
        act_ref[s] = _dot(act.astype(BF16), sel_ref[...]).astype(BF16)

    @pl.when(live & (s >= ng))
    def _():
        y = bd_ref[...] + _dot(act_ref[0], wd_ref[0:fc, :].astype(BF16))
        for c in range(1, ng):
            y = y + _dot(act_ref[c], wd_ref[c * fc:(c + 1) * fc, :].astype(BF16))
        y_ref[...] = y.astype(y_ref.dtype)

    @pl.when(jnp.logical_not(live) & (s >= ng))
    def _():
        y_ref[...] = jnp.zeros_like(y_ref)


def _experts(xs, tile_expert, n_active, w_gu, b_gu, w_down, b_down, layer):
    P, D = xs.shape
    _, E, _, F2 = w_gu.shape
    F = F2 // 2
    tm, fc, dc = MOE_TM, min(MOE_FC, F), min(MOE_DC, D)
    ng, nd = F // fc, D // dc
    nt = P // tm
    even = (lax.broadcasted_iota(I32, (2 * fc, fc), 0) == 2 * lax.broadcasted_iota(I32, (2 * fc, fc), 1)).astype(BF16)

    def live(i, na):
        return jnp.minimum(i, jnp.maximum(na[0], 1) - 1)

    def g_chunk(i, s, na):
        return jnp.where(i < na[0], jnp.minimum(s, ng - 1), ng - 1)

    def d_chunk(i, s, na):
        return jnp.where(i < na[0], jnp.clip(s - ng, 0, nd - 1), nd - 1)

    grid_spec = pltpu.PrefetchScalarGridSpec(
        num_scalar_prefetch=2,
        grid=(nt, ng + nd),
        in_specs=[
            pl.BlockSpec((tm, D), lambda i, s, te, na: (live(i, na), 0), pipeline_mode=pl.Buffered(1)),
            pl.BlockSpec((None, None, D, 2 * fc), lambda i, s, te, na: (layer, te[live(i, na)], 0, g_chunk(i, s, na))),
            pl.BlockSpec((None, None, 1, 2 * fc), lambda i, s, te, na: (layer, te[live(i, na)], 0, g_chunk(i, s, na))),
            pl.BlockSpec((None, None, F, dc), lambda i, s, te, na: (layer, te[live(i, na)], 0, d_chunk(i, s, na))),
            pl.BlockSpec((None, None, 1, dc), lambda i, s, te, na: (layer, te[live(i, na)], 0, d_chunk(i, s, na))),
            pl.BlockSpec((2 * fc, fc), lambda i, s, te, na: (0, 0)),
        ],
        out_specs=pl.BlockSpec((tm, dc), lambda i, s, te, na: (i, jnp.clip(s - ng, 0, nd - 1))),
        scratch_shapes=[pltpu.VMEM((ng, tm, fc), BF16)],
    )
    return pl.pallas_call(
        functools.partial(_expert_kernel, ng=ng, fc=fc),
        grid_spec=grid_spec,
        out_shape=jax.ShapeDtypeStruct((P, D), BF16),
        compiler_params=_cparams(("arbitrary", "arbitrary")),
        name="moe_experts",
    )(tile_expert, n_active, xs, w_gu, b_gu.reshape(-1, E, 1, F2), w_down, b_down.reshape(-1, E, 1, D), even)


def _moe(xn, xb, top_idx, gates, w_gu, b_gu, w_down, b_down, layer, ln_g, ln_b):
    T, D = xn.shape
    E = w_gu.shape[1]
    tm = MOE_TM
    e_flat = top_idx[:, :TOP_K].reshape(-1)
    onehot = (e_flat[:, None] == jnp.arange(E, dtype=I32)[None, :]).astype(I32)
    rank = jnp.take_along_axis(jnp.cumsum(onehot, axis=0), e_flat[:, None], axis=1)[:, 0] - 1
    counts = jnp.sum(onehot, axis=0)
    padded = ((counts + tm - 1) // tm) * tm
    ends = jnp.cumsum(padded)
    starts = ends - padded
    slot = starts[e_flat] + rank
    n_tiles = (T * TOP_K + E * (tm - 1)) // tm
    P = n_tiles * tm
    token_of_slot = (jnp.arange(P, dtype=I32) % T).at[slot].set(jnp.arange(T * TOP_K, dtype=I32) // TOP_K)
    tile_expert = jnp.minimum(jnp.searchsorted(ends, jnp.arange(n_tiles, dtype=I32) * tm, side="right"),
                              E - 1).astype(I32)
    n_active = (ends[-1:] // tm).astype(I32)

    xs = jnp.take(xb, token_of_slot, axis=0, mode="clip")
    y = _experts(xs, tile_expert, n_active, w_gu, b_gu, w_down, b_down, layer)
    slot_km = slot.reshape(T, TOP_K).T.reshape(-1)
    y4 = jnp.take(y, slot_km, axis=0, mode="clip").reshape(TOP_K, T, D)
    return _combine_ln(xn, y4, gates, ln_g, ln_b)


L1_IN_PAD = 1792


def _l1_post_kernel(h_ref, gq_ref, gkv_ref, gk_ref, bk_ref, c_ref, s_ref, cq_ref, kc_ref, ki_ref, wi_ref):
    def rms(v, g):
        return v * lax.rsqrt(jnp.mean(v * v, axis=1, keepdims=True) + 1e-6) * g

    cq_ref[...] = rms(h_ref[:, :Q_LORA], gq_ref[...]).astype(BF16)
    kc_ref[:, :KV_LORA] = rms(h_ref[:, Q_LORA:Q_LORA + KV_LORA], gkv_ref[...]).astype(BF16)
    c, s = c_ref[...], s_ref[...]
    o_idx = Q_LORA + KV_LORA
    ki = _layernorm(h_ref[:, o_idx:o_idx + IDX_DIM], gk_ref[...], bk_ref[...])
    ki_ref[...] = _rope128(ki, c, s, _swap_pair).astype(BF16)
    tail = h_ref[:, o_idx + IDX_DIM:]
    kc_ref[:, KV_LORA:] = _rope128(tail, c, s, _swap_pair)[:, :QK_ROPE].astype(BF16)
    wi_ref[...] = tail * (IDX_HEADS ** -0.5 * IDX_DIM ** -0.5)


def _l1_post(h1, g_q, g_kv, g_kidx, b_kidx, idx_tab, tm=256):
    T = h1.shape[0]
    row = lambda n: pl.BlockSpec((tm, n), lambda i: (i, 0))
    const = lambda n: pl.BlockSpec((1, n), lambda i: (0, 0))
    vec = lambda a: a.reshape(1, -1).astype(F32)
    return pl.pallas_call(
        _l1_post_kernel,
        grid=(T // tm,),
        in_specs=[row(L1_IN_PAD), const(Q_LORA), const(KV_LORA), const(IDX_DIM), const(IDX_DIM),
                  row(LANES), row(LANES)],
        out_specs=[row(Q_LORA), row(KV_LORA + QK_ROPE), row(IDX_DIM), row(LANES)],
        out_shape=[jax.ShapeDtypeStruct((T, Q_LORA), BF16), jax.ShapeDtypeStruct((T, KV_LORA + QK_ROPE), BF16),
                   jax.ShapeDtypeStruct((T, IDX_DIM), BF16), jax.ShapeDtypeStruct((T, LANES), F32)],
        compiler_params=_cparams(("parallel",)),
        name="l1_post",
    )(h1, vec(g_q), vec(g_kv), vec(g_kidx), vec(b_kidx), idx_tab[0], idx_tab[1])


def _qlat_kernel(qn_ref, wuk_ref, qpe_ref, o_ref):
    ql = _dot_nt(qn_ref[...], wuk_ref[...].astype(BF16))
    o_ref[:, :KV_LORA] = ql.astype(BF16)
    o_ref[:, KV_LORA:] = qpe_ref[...]


def _q_absorb(q_nope, w_uk, q_pe_h, tm=1024):
    T = q_nope.shape[0]
    tm = min(tm, T)
    H = MLA_HEADS
    return pl.pallas_call(
        _qlat_kernel,
        grid=(H, T // tm),
        in_specs=[pl.BlockSpec((tm, QK_NOPE), lambda h, i: (i, h)),
                  pl.BlockSpec((KV_LORA, QK_NOPE), lambda h, i: (0, h)),
                  pl.BlockSpec((None, tm, QK_ROPE), lambda h, i: (h, i, 0))],
        out_specs=pl.BlockSpec((None, tm, KV_LORA + QK_ROPE), lambda h, i: (h, i, 0)),
        out_shape=jax.ShapeDtypeStruct((H, T, KV_LORA + QK_ROPE), BF16),
        compiler_params=_cparams(("parallel", "parallel")),
        name="q_absorb",
    )(q_nope, w_uk, q_pe_h)


DSA_TQ = 64
DSA_KB = 256
DSA_TK = 512
DSA_HG = 8


def _indexer_kernel(qi_ref, w_ref, ki_ref, bias_ref, key_ref, *, T, tq, n_sel):
    i = pl.program_id(0)
    kb = DSA_KB
    nkb = T // kb
    n_live = ((i + 1) * tq + kb - 1) // kb
    q = qi_ref[...].reshape(IDX_HEADS * tq, IDX_DIM)
    w = w_ref[...]

    def score_block(j, carry):
        rows = pl.ds(pl.multiple_of(j * kb, kb), kb)
        s = _dot_nt(q, ki_ref[rows, :])
        isc = jnp.sum(jnp.maximum(s, 0.0).reshape(IDX_HEADS, tq, kb) * w, axis=0) + 0.0
        qpos = i * tq + lax.broadcasted_iota(I32, (tq, kb), 0)
        kpos = j * kb + lax.broadcasted_iota(I32, (tq, kb), 1)
        isc = jnp.where(kpos <= qpos, isc, -jnp.inf)
        bits = pltpu.bitcast(isc, I32)
        key_ref[j] = jnp.where(bits < 0, bits ^ 0x7FFFFFFF, bits)
        return carry

    lax.fori_loop(0, n_live, score_block, 0)

    def count_ge(cand):
        cb = jnp.broadcast_to(cand, (tq, LANES))

        def body(j, c):
            blk = key_ref[j]
            for g in range(kb // LANES):
                c = c + (blk[:, g * LANES:(g + 1) * LANES] >= cb).astype(F32)
            return c

        c = lax.fori_loop(0, n_live, body, jnp.zeros((tq, LANES), F32))
        return jnp.sum(c, axis=1, keepdims=True)

    thr = jnp.full((tq, 1), -2**31, I32)
    for bit in range(31, -1, -1):
        cand = jnp.zeros((tq, 1), I32) if bit == 31 else thr + (1 << bit)
        thr = jnp.where(count_ge(cand) >= n_sel, cand, thr)

    def write_block(j, carry):
        blk = key_ref[j]
        bias_ref[j] = jnp.where((blk >= thr) & (blk > KEY_NEG_INF), 0.0, NEG).astype(BF16)
        return carry

    lax.fori_loop(0, n_live, write_block, 0)

    def dead_block(j, carry):
        bias_ref[j] = jnp.full((tq, kb), NEG, BF16)
        return carry

    lax.fori_loop(n_live, nkb, dead_block, 0)


def _indexer_mask(qi_h, w_col, ki, T, n_sel):
    tq, kb = min(DSA_TQ, T), DSA_KB
    nkb = T // kb
    return pl.pallas_call(
        functools.partial(_indexer_kernel, T=T, tq=tq, n_sel=n_sel),
        grid=(T // tq,),
        in_specs=[pl.BlockSpec((IDX_HEADS, tq, IDX_DIM), lambda i: (0, i, 0)),
                  pl.BlockSpec((IDX_HEADS, tq, 1), lambda i: (0, i, 0)),
                  pl.BlockSpec((T, IDX_DIM), lambda i: (0, 0), pipeline_mode=pl.Buffered(1))],
        out_specs=pl.BlockSpec((None, nkb, tq, kb), lambda i: (i, 0, 0, 0)),
        out_shape=jax.ShapeDtypeStruct((T // tq, nkb, tq, kb), BF16),
        scratch_shapes=[pltpu.VMEM((nkb, tq, kb), I32)],
        compiler_params=_cparams(("parallel",)),
        name="dsa_indexer",
    )(qi_h, w_col, ki)


def _dsa_kernel(q_ref, bias_ref, kc_ref, wuv_ref, o_ref, s_ref, p_ref, a_ref, m_ref, l_ref, acc_ref, *, tq, scale):
    i = pl.program_id(0)
    H, tk = MLA_HEADS, DSA_TK
    R = H * tq
    per = tk // DSA_KB
    n_steps = ((i + 1) * tq + tk - 1) // tk
    c = scale * LOG2E
    q = q_ref[...].reshape(R, KV_LORA + QK_ROPE)

    def keys(j):
        return kc_ref[pl.ds(pl.multiple_of(j * tk, tk), tk), :]

    m_ref[...] = jnp.full_like(m_ref, NEG)
    l_ref[...] = jnp.zeros_like(l_ref)
    acc_ref[...] = jnp.zeros_like(acc_ref)
    p_ref[1] = jnp.zeros((R, tk), BF16)
    a_ref[1] = jnp.ones((R, LANES), F32)
    s_ref[0] = _dot_nt(q, keys(0))
    lane_tiles = [slice(t * LANES, (t + 1) * LANES) for t in range(tk // LANES)]

    def add_values(slot, j):
        pv = _dot(p_ref[slot], keys(j)[:, :KV_LORA])
        a = a_ref[slot]
        for t in range(KV_LORA // LANES):
            cols = slice(t * LANES, (t + 1) * LANES)
            acc_ref[:, cols] = a * acc_ref[:, cols] + pv[:, cols]

    def step(j, carry):
        cur = j % 2
        s_ref[1 - cur] = _dot_nt(q, keys(jnp.minimum(j + 1, n_steps - 1)))
        add_values(1 - cur, jnp.maximum(j - 1, 0))
        bias = jnp.concatenate([bias_ref[j * per + b] for b in range(per)], axis=1).astype(F32)
        for h in range(H):
            r = slice(h * tq, (h + 1) * tq)
            s = s_ref[cur, r, :] + bias
            m_prev = m_ref[r, :]
            m_new = jnp.maximum(m_prev, jnp.max(s, axis=1, keepdims=True))
            alpha = jnp.exp2((m_prev - m_new) * c)
            p = [jnp.exp2((s[:, t] - m_new) * c) for t in lane_tiles]
            l_ref[r, :] = alpha * l_ref[r, :] + sum(p[1:], p[0])
            m_ref[r, :] = m_new
            a_ref[cur, r, :] = alpha
            for t, pt in zip(lane_tiles, p):
                p_ref[cur, r, t] = pt.astype(BF16)
        return carry

    lax.fori_loop(0, n_steps, step, 0)
    add_values((n_steps - 1) % 2, n_steps - 1)

    for h in range(H):
        r = slice(h * tq, (h + 1) * tq)
        l = jnp.sum(l_ref[r, :], axis=1, keepdims=True)
        o_lat = (acc_ref[r, :] / l).astype(BF16)
        o_ref[:, h * V_HEAD:(h + 1) * V_HEAD] = _dot(o_lat, wuv_ref[h]).astype(o_ref.dtype)


def _dsa_attention(q576, bias, kc, w_uv_h, T):
    tq = min(DSA_TQ, T)
    H = MLA_HEADS
    C = KV_LORA + QK_ROPE
    nkb = T // DSA_KB
    assert T % DSA_TK == 0
    return pl.pallas_call(
        functools.partial(_dsa_kernel, tq=tq, scale=(QK_NOPE + QK_ROPE) ** -0.5),
        grid=(T // tq,),
        in_specs=[pl.BlockSpec((H, tq, C), lambda i: (0, i, 0)),
                  pl.BlockSpec((None, nkb, tq, DSA_KB), lambda i: (i, 0, 0, 0)),
                  pl.BlockSpec((T, C), lambda i: (0, 0), pipeline_mode=pl.Buffered(1)),
                  pl.BlockSpec((H, KV_LORA, V_HEAD), lambda i: (0, 0, 0), pipeline_mode=pl.Buffered(1))],
        out_specs=pl.BlockSpec((tq, H * V_HEAD), lambda i: (i, 0)),
        out_shape=jax.ShapeDtypeStruct((T, H * V_HEAD), BF16),
        scratch_shapes=[pltpu.VMEM((2, H * tq, DSA_TK), F32), pltpu.VMEM((2, H * tq, DSA_TK), BF16),
                        pltpu.VMEM((2, H * tq, LANES), F32),
                        pltpu.VMEM((H * tq, LANES), F32), pltpu.VMEM((H * tq, LANES), F32),
                        pltpu.VMEM((H * tq, KV_LORA), F32)],
        compiler_params=_cparams(("parallel",)),
        name="dsa_attention",
    )(q576, bias, kc, w_uv_h)


def _even_mixer(xb, T, w_in, w_out, lq1, lk1, lq2, lk2, g_subln, layer_idx, tabs):
    full = tabs[0]
    rope = [(0, 2 * MOBA_W, full, _swap_full),
            (3 * MOBA_W, 3 * MOBA_W + 2 * DIFF_W, full, _swap_full)]
    h0 = _matmul(xb, w_in, tm=2048, tn=1024, tk=1024, out_dtype=BF16, rope=rope, name="l0_in_proj")
    o_moba = _moba_attention(h0, T)
    o_diff = _diff_attention(h0, T, lq1, lk1, lq2, lk2, g_subln, layer_idx)
    o = jnp.concatenate([o_moba, o_diff], axis=1)
    return _matmul(o, w_out, tm=2048, tn=1024, tk=512, out_dtype=F32, name="l0_out_proj")


def _odd_mixer(xb, T, w_in, g_q, g_kv, w_qb, w_uk, w_uv, w_iq, g_kidx, b_kidx, w_out, tabs):
    _, pair, idx = tabs
    D = w_in.shape[0]
    H = MLA_HEADS
    o_kpe, o_kidx, o_w = Q_LORA + KV_LORA, Q_LORA + KV_LORA + QK_ROPE, Q_LORA + KV_LORA + QK_ROPE + IDX_DIM
    w_in_p = jnp.concatenate([w_in[:, :o_kpe], w_in[:, o_kidx:o_w], w_in[:, o_kpe:o_kidx], w_in[:, o_w:],
                              jnp.zeros((D, L1_IN_PAD - w_in.shape[1]), w_in.dtype)], axis=1)
    h1 = _matmul(xb, w_in_p, tm=2048, tn=L1_IN_PAD // 2, tk=512, out_dtype=F32, name="l1_in_proj")
    cq, kc, ki, wi_full = _l1_post(h1, g_q, g_kv, g_kidx, b_kidx, idx)
    w_qb3 = w_qb.reshape(Q_LORA, H, QK_NOPE + QK_ROPE)
    w_qb_p = jnp.concatenate([w_qb3[:, :, :QK_NOPE].reshape(Q_LORA, H * QK_NOPE),
                              w_qb3[:, :, QK_NOPE:].reshape(Q_LORA, H * QK_ROPE)], axis=1)
    n0 = H * QK_NOPE
    q = _matmul(cq, w_qb_p, tm=2048, tn=1024, tk=Q_LORA, out_dtype=BF16,
                rope=[(n0, n0 + H * QK_ROPE, pair, _swap_pair)], name="l1_q_proj")
    qi = _matmul(cq, w_iq, tm=2048, tn=1024, tk=Q_LORA, out_dtype=BF16,
                 rope=[(0, IDX_HEADS * IDX_DIM, idx, _swap_pair)], name="l1_qi_proj")
    q_pe_h = q[:, n0:].reshape(T, H, QK_ROPE).transpose(1, 0, 2)
    q576 = _q_absorb(q, w_uk, q_pe_h)
    qi_h = qi.reshape(T, IDX_HEADS, IDX_DIM).transpose(1, 0, 2)
    w_col = wi_full[:, QK_ROPE:QK_ROPE + IDX_HEADS].T[:, :, None]
    bias = _indexer_mask(qi_h, w_col, ki, T, min(IDX_TOPK_MAX, T // 4))
    w_uv_h = w_uv.reshape(KV_LORA, H, V_HEAD).transpose(1, 0, 2).astype(BF16)
    o = _dsa_attention(q576, bias, kc, w_uv_h, T)
    return _matmul(o, w_out, tm=2048, tn=1024, tk=512, out_dtype=F32, name="l1_out_proj")


def kernel(x, l0_w_in, l0_w_out, l0_lam_q1, l0_lam_k1, l0_lam_q2, l0_lam_k2, l0_g_subln, l1_w_in, l1_g_q, l1_g_kv, l1_w_qb, l1_w_uk, l1_w_uv, l1_w_iq, l1_g_kidx, l1_b_kidx, l1_w_out, ln_g, ln_b, moe_w_router, moe_b_router, moe_w_gu, moe_b_gu, moe_w_down, moe_b_down):
    B, T, D = x.shape
    assert B == 1
    xn = x.reshape(T, D)
    xb = xn.astype(BF16)
    tabs = _rope_tables(T)
    for i in range(DEPTH):
        if i % 2 == 0:
            mix = _even_mixer(xb, T, l0_w_in, l0_w_out, l0_lam_q1, l0_lam_k1, l0_lam_q2, l0_lam_k2,
                              l0_g_subln, i, tabs)
        else:
            mix = _odd_mixer(xb, T, l1_w_in, l1_g_q, l1_g_kv, l1_w_qb, l1_w_uk, l1_w_uv, l1_w_iq,
                             l1_g_kidx, l1_b_kidx, l1_w_out, tabs)
        xn, xb, top_idx, gates = _mix_ln_route(xn, mix, ln_g[i, 0], ln_b[i, 0], moe_w_router[i], moe_b_router[i])
        xn, xb = _moe(xn, xb, top_idx, gates, moe_w_gu, moe_b_gu, moe_w_down, moe_b_down, i,
                      ln_g[i, 1], ln_b[i, 1])
    return xn.reshape(B, T, D)
```

```python
import functools
import math

import jax
import jax.numpy as jnp
from jax import lax
from jax.experimental import pallas as pl
from jax.experimental.pallas import tpu as pltpu

F32 = jnp.float32
BF16 = jnp.bfloat16
I32 = jnp.int32

HEAD_DIM = 128
ROPE_THETA = 10000.0
MOBA_HEADS = 16
MOBA_BLOCK = 256
MOBA_TOPK = 3
DIFF_HEADS = 8
MOBA_W = MOBA_HEADS * HEAD_DIM
DIFF_W = DIFF_HEADS * 2 * HEAD_DIM
MLA_HEADS = 32
Q_LORA = 1024
KV_LORA = 512
QK_NOPE = 128
QK_ROPE = 64
V_HEAD = 128
IDX_HEADS = 32
IDX_DIM = 128
IDX_TOPK_MAX = 256
N_EXPERTS = 32
TOP_K = 4
SWIGLU_LIMIT = 7.0
SWIGLU_ALPHA = 1.702
DEPTH = 2
DN_ALPHA = (2 * DEPTH) ** 0.25
LN_EPS = 1e-5

LANES = 128
VMEM_LIMIT_BYTES = 56 * 2**20

NEG = -1e30
LOG2E = 1.4426950408889634
KEY_NEG_INF = -2139095041


def _cparams(semantics, flags=None):
    return pltpu.CompilerParams(dimension_semantics=semantics, vmem_limit_bytes=VMEM_LIMIT_BYTES, flags=flags)


def _dot(a, b):
    return jnp.dot(a, b, preferred_element_type=F32)


def _dot_nt(a, b, precision=None):
    return lax.dot_general(a, b, (((1,), (1,)), ((), ())), preferred_element_type=F32, precision=precision)


def _rope_tables(T):
    pos = jnp.arange(T).astype(F32)[:, None]
    inv64 = ROPE_THETA ** (-jnp.arange(64, dtype=F32) / 64)
    a64 = pos * inv64[None, :]
    c64, s64 = jnp.cos(a64), jnp.sin(a64)
    inv32 = ROPE_THETA ** (-jnp.arange(32, dtype=F32) / 32)
    a32 = pos * inv32[None, :]
    c32, s32 = jnp.cos(a32), jnp.sin(a32)
    one, zero = jnp.ones_like(c32), jnp.zeros_like(c32)
    full = (jnp.concatenate([c64, c64], 1), jnp.concatenate([-s64, s64], 1))
    pair = (jnp.concatenate([c32, c32, c32, c32], 1), jnp.concatenate([-s32, s32, -s32, s32], 1))
    idx = (jnp.concatenate([c32, c32, one, one], 1), jnp.concatenate([-s32, s32, zero, zero], 1))
    return full, pair, idx


def _swap_full(z):
    return pltpu.roll(z, 64, axis=1)


def _swap_pair(z):
    lane = lax.broadcasted_iota(I32, z.shape, 1)
    return jnp.where((lane % 64) < 32, pltpu.roll(z, 96, axis=1), pltpu.roll(z, 32, axis=1))


def _rope128(z, c, s, swap):
    return z * c + swap(z) * s


def _mm_kernel(*refs, nk, tn, rope_ranges, swaps):
    n_tab = len(swaps)
    a_ref, b_ref = refs[0], refs[1]
    tab_refs = refs[2:2 + 2 * n_tab]
    o_ref, acc_ref = refs[2 + 2 * n_tab], refs[3 + 2 * n_tab]
    k = pl.program_id(2)

    @pl.when(k == 0)
    def _():
        acc_ref[...] = jnp.zeros_like(acc_ref)

    acc_ref[...] += _dot(a_ref[...].astype(BF16), b_ref[...].astype(BF16))

    @pl.when(k == nk - 1)
    def _():
        if not rope_ranges:
            o_ref[...] = acc_ref[...].astype(o_ref.dtype)
            return
        j = pl.program_id(1)
        plain = None
        for (lo, hi, t) in rope_ranges:
            hit = (j >= lo) & (j < hi)
            plain = hit if plain is None else (plain | hit)

            @pl.when(hit)
            def _(t=t):
                c = tab_refs[2 * t][...]
                s = tab_refs[2 * t + 1][...]
                for g in range(tn // LANES):
                    z = acc_ref[:, g * LANES:(g + 1) * LANES]
                    o_ref[:, g * LANES:(g + 1) * LANES] = _rope128(z, c, s, swaps[t]).astype(o_ref.dtype)

        @pl.when(jnp.logical_not(plain))
        def _():
            o_ref[...] = acc_ref[...].astype(o_ref.dtype)


def _matmul(a, b, *, tm, tn, tk, out_dtype, rope=None, name="mm"):
    M, K = a.shape
    _, N = b.shape
    tm, tn, tk = min(tm, M), min(tn, N), min(tk, K)
    assert M % tm == 0 and N % tn == 0 and K % tk == 0
    rope = rope or []
    ranges, tabs, swaps = [], [], []
    for t, (lo, hi, (c, s), swap) in enumerate(rope):
        assert lo % tn == 0 and hi % tn == 0 and tn % LANES == 0
        ranges.append((lo // tn, hi // tn, t))
        tabs += [c, s]
        swaps.append(swap)
    nk = K // tk
    in_specs = [pl.BlockSpec((tm, tk), lambda i, j, k: (i, k)),
                pl.BlockSpec((tk, tn), lambda i, j, k: (k, j))]
    in_specs += [pl.BlockSpec((tm, LANES), lambda i, j, k: (i, 0)) for _ in tabs]
    return pl.pallas_call(
        functools.partial(_mm_kernel, nk=nk, tn=tn, rope_ranges=tuple(ranges), swaps=tuple(swaps)),
        grid=(M // tm, N // tn, nk),
        in_specs=in_specs,
        out_specs=pl.BlockSpec((tm, tn), lambda i, j, k: (i, j)),
        out_shape=jax.ShapeDtypeStruct((M, N), out_dtype),
        scratch_shapes=[pltpu.VMEM((tm, tn), F32)],
        compiler_params=_cparams(("parallel", "parallel", "arbitrary")),
        name=name,
    )(a, b, *tabs)


SOFTMAX_ROWS = 64


def _softmax_step(s, v, m_ref, l_ref, acc_ref, scale):
    c = scale * LOG2E
    tq, tk = s.shape
    dv = v.shape[1]
    rows = [slice(r, r + SOFTMAX_ROWS) for r in range(0, tq, SOFTMAX_ROWS)]
    p_rows, alphas = [], []
    for r in rows:
        sc = s[r, :]
        m_prev = m_ref[r, :]
        m_new = jnp.maximum(m_prev, jnp.max(sc, axis=1, keepdims=True))
        alpha = jnp.exp2((m_prev - m_new) * c)
        p = [jnp.exp2((sc[:, t:t + LANES] - m_new) * c) for t in range(0, tk, LANES)]
        l_ref[r, :] = alpha * l_ref[r, :] + sum(p[1:], p[0])
        m_ref[r, :] = m_new
        p_rows.append(jnp.concatenate([pt.astype(BF16) for pt in p], axis=1))
        alphas.append(alpha)
    pv = _dot(jnp.concatenate(p_rows, axis=0), v)
    for r, alpha in zip(rows, alphas):
        for t in range(0, dv, LANES):
            acc_ref[r, t:t + LANES] = alpha * acc_ref[r, t:t + LANES] + pv[r, t:t + LANES]


def _row_sum(l_ref):
    return jnp.sum(l_ref[...], axis=1, keepdims=True)


def _causal_mask(s, row0, col0):
    row = row0 + lax.broadcasted_iota(I32, s.shape, 0)
    col = col0 + lax.broadcasted_iota(I32, s.shape, 1)
    return jnp.where(col <= row, s, NEG)


MOBA_TILE = 512


def _moba_kernel(q_ref, k_ref, v_ref, o_ref, kmean_ref, m_ref, l_ref, acc_ref, *, T, tq, scale):
    i = pl.program_id(1)
    nb = T // MOBA_BLOCK
    per = tq // MOBA_BLOCK

    @pl.when(i == 0)
    def _():
        blk = lax.broadcasted_iota(I32, (LANES, T), 0)
        pos = lax.broadcasted_iota(I32, (LANES, T), 1)
        avg = jnp.where(pos // MOBA_BLOCK == blk, 1.0 / MOBA_BLOCK, 0.0).astype(BF16)
        kmean_ref[...] = _dot(avg, k_ref[...])

    q = q_ref[...]
    gate = _dot_nt(q.astype(F32), kmean_ref[...], precision=lax.Precision.HIGHEST)
    lane = lax.broadcasted_iota(I32, (tq, LANES), 1)
    lane_f = lane.astype(F32)
    own = i * per + lax.broadcasted_iota(I32, (tq, LANES), 0) // MOBA_BLOCK
    past = lane < own
    g = jnp.where(past, gate, -jnp.inf)
    sel = jnp.zeros((tq, LANES), jnp.bool_)
    for _ in range(min(MOBA_TOPK, nb)):
        mx = jnp.max(g, axis=1, keepdims=True)
        first = jnp.min(jnp.where(g == mx, lane_f, float(LANES)), axis=1, keepdims=True)
        pick = lane_f == first
        sel = sel | pick
        g = jnp.where(pick, -jnp.inf, g)
    visible = (sel & past) | (lane == own)
    q_aug = jnp.concatenate([q, jnp.where(visible, 0.0, NEG).astype(BF16)], axis=1)

    m_ref[...] = jnp.full_like(m_ref, NEG)
    l_ref[...] = jnp.zeros_like(l_ref)
    acc_ref[...] = jnp.zeros_like(acc_ref)
    key_blk = lax.broadcasted_iota(I32, (tq, LANES), 0) // MOBA_BLOCK
    key_lane = lax.broadcasted_iota(I32, (tq, LANES), 1)

    def tile(j, causal):
        rows = pl.ds(pl.multiple_of(j * tq, tq), tq)
        k_aug = jnp.concatenate([k_ref[rows, :], (key_lane == j * per + key_blk).astype(BF16)], axis=1)
        s = _dot_nt(q_aug, k_aug)
        if causal:
            s = _causal_mask(s, 0, 0)
        _softmax_step(s, v_ref[rows, :], m_ref, l_ref, acc_ref, scale)

    def past_tile(j, carry):
        tile(j, False)
        return carry

    lax.fori_loop(0, i, past_tile, 0)
    tile(i, True)
    o_ref[...] = (acc_ref[...] / _row_sum(l_ref)).astype(o_ref.dtype)


def _moba_attention(h0, T):
    tq = min(MOBA_TILE, T)
    assert T % tq == 0 and tq % MOBA_BLOCK == 0 and T // MOBA_BLOCK <= LANES
    H = MOBA_HEADS
    return pl.pallas_call(
        functools.partial(_moba_kernel, T=T, tq=tq, scale=HEAD_DIM ** -0.5),
        grid=(H, T // tq),
        in_specs=[pl.BlockSpec((tq, HEAD_DIM), lambda h, i: (i, h)),
                  pl.BlockSpec((T, HEAD_DIM), lambda h, i: (0, H + h)),
                  pl.BlockSpec((T, HEAD_DIM), lambda h, i: (0, 2 * H + h))],
        out_specs=pl.BlockSpec((tq, HEAD_DIM), lambda h, i: (i, h)),
        out_shape=jax.ShapeDtypeStruct((T, MOBA_W), BF16),
        scratch_shapes=[pltpu.VMEM((LANES, HEAD_DIM), F32),
                        pltpu.VMEM((tq, LANES), F32), pltpu.VMEM((tq, LANES), F32),
                        pltpu.VMEM((tq, HEAD_DIM), F32)],
        compiler_params=_cparams(("parallel", "arbitrary")),
        name="moba_attention",
    )(h0, h0, h0)


DIFF_TQ = 512


def _diff_kernel(lq1_ref, lk1_ref, lq2_ref, lk2_ref, g_ref, q_ref, k_ref, v_ref, o_ref,
                 m_ref, l_ref, acc_ref, *, tq, scale, lam_init):
    i = pl.program_id(1)
    m_ref[...] = jnp.full_like(m_ref, NEG)
    l_ref[...] = jnp.zeros_like(l_ref)
    acc_ref[...] = jnp.zeros_like(acc_ref)
    q = q_ref[...]

    def block(j, masked):
        rows = pl.ds(pl.multiple_of(j * tq, tq), tq)
        kj = k_ref[rows, :]
        vj = v_ref[rows, :]
        cols = [slice(mp * HEAD_DIM, (mp + 1) * HEAD_DIM) for mp in range(2)]
        scores = [_dot_nt(q[:, c], kj[:, c]) for c in cols]
        for mp in range(2):
            s = _causal_mask(scores[mp], 0, 0) if masked else scores[mp]
            _softmax_step(s, vj, m_ref.at[mp], l_ref.at[mp], acc_ref.at[mp], scale)

    def past_block(j, carry):
        block(j, False)
        return carry

    lax.fori_loop(0, i, past_block, 0)
    block(i, True)

    lam = (jnp.exp(jnp.sum(lq1_ref[...] * lk1_ref[...], axis=1, keepdims=True))
           - jnp.exp(jnp.sum(lq2_ref[...] * lk2_ref[...], axis=1, keepdims=True)) + lam_init)
    o = acc_ref[0] / _row_sum(l_ref.at[0]) - lam * (acc_ref[1] / _row_sum(l_ref.at[1]))
    o = o * lax.rsqrt(jnp.mean(o * o, axis=1, keepdims=True) + 1e-5) * g_ref[...]
    o_ref[...] = (o * (1.0 - lam_init)).astype(o_ref.dtype)


def _diff_attention(h0, T, lq1, lk1, lq2, lk2, g_subln, layer_idx):
    tq = min(DIFF_TQ, T)
    assert T % tq == 0
    W = 2 * HEAD_DIM
    q0, k0, v0 = 3 * MOBA_W // W, (3 * MOBA_W + DIFF_W) // W, (3 * MOBA_W + 2 * DIFF_W) // W
    lam_init = 0.8 - 0.6 * math.exp(-0.3 * layer_idx)
    vec = lambda a: a.reshape(1, -1).astype(F32)
    small = lambda n: pl.BlockSpec((1, n), lambda h, i: (0, 0))
    return pl.pallas_call(
        functools.partial(_diff_kernel, tq=tq, scale=HEAD_DIM ** -0.5, lam_init=lam_init),
        grid=(DIFF_HEADS, T // tq),
        in_specs=[small(HEAD_DIM)] * 4 + [small(W),
                  pl.BlockSpec((tq, W), lambda h, i: (i, q0 + h)),
                  pl.BlockSpec((T, W), lambda h, i: (0, k0 + h)),
                  pl.BlockSpec((T, W), lambda h, i: (0, v0 + h))],
        out_specs=pl.BlockSpec((tq, W), lambda h, i: (i, h)),
        out_shape=jax.ShapeDtypeStruct((T, DIFF_W), BF16),
        scratch_shapes=[pltpu.VMEM((2, tq, LANES), F32), pltpu.VMEM((2, tq, LANES), F32),
                        pltpu.VMEM((2, tq, W), F32)],
        compiler_params=_cparams(("parallel", "arbitrary")),
        name="diff_attention",
    )(vec(lq1), vec(lk1), vec(lq2), vec(lk2), vec(g_subln), h0, h0, h0)


def _layernorm(v, g, b):
    mu = jnp.mean(v, axis=1, keepdims=True)
    d = v - mu
    var = jnp.mean(d * d, axis=1, keepdims=True)
    return d * lax.rsqrt(var + LN_EPS) * g + b


def _route(xn, wr_ref, br_ref, idx_ref, gate_ref):
    logits = jnp.dot(xn, wr_ref[...], preferred_element_type=F32,
                     precision=lax.Precision.HIGHEST) + br_ref[...]
    lane = lax.broadcasted_iota(I32, logits.shape, 1)
    lane_f = lane.astype(F32)
    idx_out = jnp.zeros(logits.shape, F32)
    val_out = jnp.full(logits.shape, NEG, F32)
    g = logits
    for r in range(TOP_K):
        mx = jnp.max(g, axis=1, keepdims=True)
        first = jnp.min(jnp.where(g == mx, lane_f, float(LANES)), axis=1, keepdims=True)
        idx_out = jnp.where(lane == r, first, idx_out)
        val_out = jnp.where(lane == r, mx, val_out)
        g = jnp.where(lane_f == first, -jnp.inf, g)
    e = jnp.exp(val_out - jnp.max(val_out, axis=1, keepdims=True))
    gate_ref[...] = e / jnp.sum(e, axis=1, keepdims=True)
    idx_ref[...] = idx_out.astype(I32)


def _mix_ln_kernel(x_ref, mix_ref, g_ref, b_ref, wr_ref, br_ref, xn_ref, xb_ref, idx_ref, gate_ref):
    xn = _layernorm(DN_ALPHA * x_ref[...] + mix_ref[...], g_ref[...], b_ref[...])
    xn_ref[...] = xn
    xb_ref[...] = xn.astype(BF16)
    _route(xn, wr_ref, br_ref, idx_ref, gate_ref)


def _mix_ln_route(x, mix, g, b, w_router, b_router, tm=256):
    T, D = x.shape
    E = w_router.shape[1]
    wr = jnp.pad(w_router.astype(F32), ((0, 0), (0, LANES - E)))
    br = jnp.pad(b_router.astype(F32).reshape(1, E), ((0, 0), (0, LANES - E)), constant_values=NEG)
    row = lambda n: pl.BlockSpec((tm, n), lambda i: (i, 0))
    const = lambda r, n: pl.BlockSpec((r, n), lambda i: (0, 0))
    return pl.pallas_call(
        _mix_ln_kernel,
        grid=(T // tm,),
        in_specs=[row(D), row(D), const(1, D), const(1, D), const(D, LANES), const(1, LANES)],
        out_specs=[row(D), row(D), row(LANES), row(LANES)],
        out_shape=[jax.ShapeDtypeStruct((T, D), F32), jax.ShapeDtypeStruct((T, D), BF16),
                   jax.ShapeDtypeStruct((T, LANES), I32), jax.ShapeDtypeStruct((T, LANES), F32)],
        compiler_params=_cparams(("parallel",)),
        name="mix_ln_route",
    )(x, mix, g.reshape(1, D).astype(F32), b.reshape(1, D).astype(F32), wr, br)


def _combine_ln_kernel(x_ref, y_ref, gate_ref, g_ref, b_ref, xn_ref, xb_ref):
    gates = gate_ref[...]
    f = jnp.zeros(x_ref.shape, F32)
    for r in range(TOP_K):
        f = f + gates[:, r:r + 1] * y_ref[r].astype(F32)
    xn = _layernorm(DN_ALPHA * x_ref[...] + f, g_ref[...], b_ref[...])
    xn_ref[...] = xn
    xb_ref[...] = xn.astype(BF16)


def _combine_ln(x, y4, gates, g, b, tm=256):
    T, D = x.shape
    row = lambda n: pl.BlockSpec((tm, n), lambda i: (i, 0))
    const = lambda r, n: pl.BlockSpec((r, n), lambda i: (0, 0))
    return pl.pallas_call(
        _combine_ln_kernel,
        grid=(T // tm,),
        in_specs=[row(D), pl.BlockSpec((TOP_K, tm, D), lambda i: (0, i, 0)), row(LANES), const(1, D), const(1, D)],
        out_specs=[row(D), row(D)],
        out_shape=[jax.ShapeDtypeStruct((T, D), F32), jax.ShapeDtypeStruct((T, D), BF16)],
        compiler_params=_cparams(("parallel",)),
        name="moe_combine_ln",
    )(x, y4, gates, g.reshape(1, D).astype(F32), b.reshape(1, D).astype(F32))


MOE_TM = 1152
MOE_FC = 256
MOE_DC = 1024


def _expert_kernel(te_ref, na_ref, x_ref, wgu_ref, bgu_ref, wd_ref, bd_ref, sel_ref, y_ref, act_ref, *, ng, fc):
    i = pl.program_id(0)
    s = pl.program_id(1)
    live = i < na_ref[0]

    @pl.when(live & (s < ng))
    def _():
        hg = _dot(x_ref[...], wgu_ref[...].astype(BF16)) + bgu_ref[...]
        gate = jnp.minimum(hg, SWIGLU_LIMIT)
        up = jnp.clip(hg, -SWIGLU_LIMIT, SWIGLU_LIMIT)
        up_next = pltpu.roll(up, 2 * fc - 1, axis=1)
        act = (up_next + 1.0) * (gate * (1.0 / (1.0 + jnp.exp(-SWIGLU_ALPHA * gate))))
        act_ref[s] = _dot(act.astype(BF16), sel_ref[...]).astype(BF16)

    @pl.when(live & (s >= ng))
    def _():
        y = bd_ref[...] + _dot(act_ref[0], wd_ref[0:fc, :].astype(BF16))
        for c in range(1, ng):
            y = y + _dot(act_ref[c], wd_ref[c * fc:(c + 1) * fc, :].astype(BF16))
        y_ref[...] = y.astype(y_ref.dtype)

    @pl.when(jnp.logical_not(live) & (s >= ng))
    def _():
        y_ref[...] = jnp.zeros_like(y_ref)


def _experts(xs, tile_expert, n_active, w_gu, b_gu, w_down, b_down, layer):
    P, D = xs.shape
    _, E, _, F2 = w_gu.shape
    F = F2 // 2
    tm, fc, dc = MOE_TM, min(MOE_FC, F), min(MOE_DC, D)
    ng, nd = F // fc, D // dc
    nt = P // tm
    even = (lax.broadcasted_iota(I32, (2 * fc, fc), 0) == 2 * lax.broadcasted_iota(I32, (2 * fc, fc), 1)).astype(BF16)

    def live(i, na):
        return jnp.minimum(i, jnp.maximum(na[0], 1) - 1)

    def g_chunk(i, s, na):
        return jnp.where(i < na[0], jnp.minimum(s, ng - 1), ng - 1)

    def d_chunk(i, s, na):
        return jnp.where(i < na[0], jnp.clip(s - ng, 0, nd - 1), nd - 1)

    grid_spec = pltpu.PrefetchScalarGridSpec(
        num_scalar_prefetch=2,
        grid=(nt, ng + nd),
        in_specs=[
            pl.BlockSpec((tm, D), lambda i, s, te, na: (live(i, na), 0), pipeline_mode=pl.Buffered(1)),
            pl.BlockSpec((None, None, D, 2 * fc), lambda i, s, te, na: (layer, te[live(i, na)], 0, g_chunk(i, s, na))),
            pl.BlockSpec((None, None, 1, 2 * fc), lambda i, s, te, na: (layer, te[live(i, na)], 0, g_chunk(i, s, na))),
            pl.BlockSpec((None, None, F, dc), lambda i, s, te, na: (layer, te[live(i, na)], 0, d_chunk(i, s, na))),
            pl.BlockSpec((None, None, 1, dc), lambda i, s, te, na: (layer, te[live(i, na)], 0, d_chunk(i, s, na))),
            pl.BlockSpec((2 * fc, fc), lambda i, s, te, na: (0, 0)),
        ],
        out_specs=pl.BlockSpec((tm, dc), lambda i, s, te, na: (i, jnp.clip(s - ng, 0, nd - 1))),
        scratch_shapes=[pltpu.VMEM((ng, tm, fc), BF16)],
    )
    return pl.pallas_call(
        functools.partial(_expert_kernel, ng=ng, fc=fc),
        grid_spec=grid_spec,
        out_shape=jax.ShapeDtypeStruct((P, D), BF16),
        compiler_params=_cparams(("arbitrary", "arbitrary")),
        name="moe_experts",
    )(tile_expert, n_active, xs, w_gu, b_gu.reshape(-1, E, 1, F2), w_down, b_down.reshape(-1, E, 1, D), even)


def _moe(xn, xb, top_idx, gates, w_gu, b_gu, w_down, b_down, layer, ln_g, ln_b):
    T, D = xn.shape
    E = w_gu.shape[1]
    tm = MOE_TM
    e_flat = top_idx[:, :TOP_K].reshape(-1)
    onehot = (e_flat[:, None] == jnp.arange(E, dtype=I32)[None, :]).astype(I32)
    rank = jnp.take_along_axis(jnp.cumsum(onehot, axis=0), e_flat[:, None], axis=1)[:, 0] - 1
    counts = jnp.sum(onehot, axis=0)
    padded = ((counts + tm - 1) // tm) * tm
    ends = jnp.cumsum(padded)
    starts = ends - padded
    slot = starts[e_flat] + rank
    n_tiles = (T * TOP_K + E * (tm - 1)) // tm
    P = n_tiles * tm
    token_of_slot = (jnp.arange(P, dtype=I32) % T).at[slot].set(jnp.arange(T * TOP_K, dtype=I32) // TOP_K)
    tile_expert = jnp.minimum(jnp.searchsorted(ends, jnp.arange(n_tiles, dtype=I32) * tm, side="right"),
                              E - 1).astype(I32)
    n_active = (ends[-1:] // tm).astype(I32)

    xs = jnp.take(xb, token_of_slot, axis=0, mode="clip")
    y = _experts(xs, tile_expert, n_active, w_gu, b_gu, w_down, b_down, layer)
    slot_km = slot.reshape(T, TOP_K).T.reshape(-1)
    y4 = jnp.take(y, slot_km, axis=0, mode="clip").reshape(TOP_K, T, D)
    return _combine_ln(xn, y4, gates, ln_g, ln_b)


L1_IN_PAD = 1792


def _l1_post_kernel(h_ref, gq_ref, gkv_ref, gk_ref, bk_ref, c_ref, s_ref, cq_ref, kc_ref, ki_ref, wi_ref):
    def rms(v, g):
        return v * lax.rsqrt(jnp.mean(v * v, axis=1, keepdims=True) + 1e-6) * g

    cq_ref[...] = rms(h_ref[:, :Q_LORA], gq_ref[...]).astype(BF16)
    kc_ref[:, :KV_LORA] = rms(h_ref[:, Q_LORA:Q_LORA + KV_LORA], gkv_ref[...]).astype(BF16)
    c, s = c_ref[...], s_ref[...]
    o_idx = Q_LORA + KV_LORA
    ki = _layernorm(h_ref[:, o_idx:o_idx + IDX_DIM], gk_ref[...], bk_ref[...])
    ki_ref[...] = _rope128(ki, c, s, _swap_pair).astype(BF16)
    tail = h_ref[:, o_idx + IDX_DIM:]
    kc_ref[:, KV_LORA:] = _rope128(tail, c, s, _swap_pair)[:, :QK_ROPE].astype(BF16)
    wi_ref[...] = tail * (IDX_HEADS ** -0.5 * IDX_DIM ** -0.5)


def _l1_post(h1, g_q, g_kv, g_kidx, b_kidx, idx_tab, tm=256):
    T = h1.shape[0]
    row = lambda n: pl.BlockSpec((tm, n), lambda i: (i, 0))
    const = lambda n: pl.BlockSpec((1, n), lambda i: (0, 0))
    vec = lambda a: a.reshape(1, -1).astype(F32)
    return pl.pallas_call(
        _l1_post_kernel,
        grid=(T // tm,),
        in_specs=[row(L1_IN_PAD), const(Q_LORA), const(KV_LORA), const(IDX_DIM), const(IDX_DIM),
                  row(LANES), row(LANES)],
        out_specs=[row(Q_LORA), row(KV_LORA + QK_ROPE), row(IDX_DIM), row(LANES)],
        out_shape=[jax.ShapeDtypeStruct((T, Q_LORA), BF16), jax.ShapeDtypeStruct((T, KV_LORA + QK_ROPE), BF16),
                   jax.ShapeDtypeStruct((T, IDX_DIM), BF16), jax.ShapeDtypeStruct((T, LANES), F32)],
        compiler_params=_cparams(("parallel",)),
        name="l1_post",
    )(h1, vec(g_q), vec(g_kv), vec(g_kidx), vec(b_kidx), idx_tab[0], idx_tab[1])


def _qlat_kernel(qn_ref, wuk_ref, qpe_ref, o_ref):
    ql = _dot_nt(qn_ref[...], wuk_ref[...].astype(BF16))
    o_ref[:, :KV_LORA] = ql.astype(BF16)
    o_ref[:, KV_LORA:] = qpe_ref[...]


def _q_absorb(q_nope, w_uk, q_pe_h, tm=1024):
    T = q_nope.shape[0]
    tm = min(tm, T)
    H = MLA_HEADS
    return pl.pallas_call(
        _qlat_kernel,
        grid=(H, T // tm),
        in_specs=[pl.BlockSpec((tm, QK_NOPE), lambda h, i: (i, h)),
                  pl.BlockSpec((KV_LORA, QK_NOPE), lambda h, i: (0, h)),
                  pl.BlockSpec((None, tm, QK_ROPE), lambda h, i: (h, i, 0))],
        out_specs=pl.BlockSpec((None, tm, KV_LORA + QK_ROPE), lambda h, i: (h, i, 0)),
        out_shape=jax.ShapeDtypeStruct((H, T, KV_LORA + QK_ROPE), BF16),
        compiler_params=_cparams(("parallel", "parallel")),
        name="q_absorb",
    )(q_nope, w_uk, q_pe_h)


DSA_TQ = 64
DSA_KB = 256
DSA_TK = 512
IDX_TQ = 128


def _indexer_kernel(qi_ref, w_ref, ki_ref, bias_ref, key_ref, *, T, tq, ts, n_sel):
    i = pl.program_id(0)
    kb = DSA_KB
    nkb = T // kb
    n_live = ((i + 1) * tq + kb - 1) // kb
    q = qi_ref[...].reshape(IDX_HEADS * tq, IDX_DIM)
    w = w_ref[...]

    def score_block(j, carry):
        rows = pl.ds(pl.multiple_of(j * kb, kb), kb)
        s = _dot_nt(q, ki_ref[rows, :])
        isc = jnp.sum(jnp.maximum(s, 0.0).reshape(IDX_HEADS, tq, kb) * w, axis=0) + 0.0
        qpos = i * tq + lax.broadcasted_iota(I32, (tq, kb), 0)
        kpos = j * kb + lax.broadcasted_iota(I32, (tq, kb), 1)
        isc = jnp.where(kpos <= qpos, isc, -jnp.inf)
        bits = pltpu.bitcast(isc, I32)
        key_ref[j] = jnp.where(bits < 0, bits ^ 0x7FFFFFFF, bits)
        return carry

    lax.fori_loop(0, n_live, score_block, 0)

    parts = [slice(r, r + ts) for r in range(0, tq, ts)]

    def count_ge(cand):
        cbs = [jnp.broadcast_to(cand[r], (ts, LANES)) for r in parts]

        def body(j, cs):
            out = []
            for r, cb, c in zip(parts, cbs, cs):
                blk = key_ref[j, r, :]
                for g in range(kb // LANES):
                    c = c + (blk[:, g * LANES:(g + 1) * LANES] >= cb).astype(F32)
                out.append(c)
            return tuple(out)

        cs = lax.fori_loop(0, n_live, body, tuple(jnp.zeros((ts, LANES), F32) for _ in parts))
        return jnp.concatenate([jnp.sum(c, axis=1, keepdims=True) for c in cs], axis=0)

    thr = jnp.full((tq, 1), -2**31, I32)
    for bit in range(31, -1, -1):
        cand = jnp.zeros((tq, 1), I32) if bit == 31 else thr + (1 << bit)
        thr = jnp.where(count_ge(cand) >= n_sel, cand, thr)

    def write_block(j, carry):
        blk = key_ref[j]
        mask = jnp.where((blk >= thr) & (blk > KEY_NEG_INF), 0.0, NEG).astype(BF16)
        for t, r in enumerate(parts):
            bias_ref[t, j] = mask[r]
        return carry

    lax.fori_loop(0, n_live, write_block, 0)

    def dead_block(j, carry):
        for t in range(len(parts)):
            bias_ref[t, j] = jnp.full((ts, kb), NEG, BF16)
        return carry

    lax.fori_loop(n_live, nkb, dead_block, 0)


def _indexer_mask(qi_h, w_col, ki, T, n_sel):
    tq, ts, kb = min(IDX_TQ, T), min(DSA_TQ, T), DSA_KB
    nkb = T // kb
    assert tq % ts == 0 and T % tq == 0
    return pl.pallas_call(
        functools.partial(_indexer_kernel, T=T, tq=tq, ts=ts, n_sel=n_sel),
        grid=(T // tq,),
        in_specs=[pl.BlockSpec((IDX_HEADS, tq, IDX_DIM), lambda i: (0, i, 0)),
                  pl.BlockSpec((IDX_HEADS, tq, 1), lambda i: (0, i, 0)),
                  pl.BlockSpec((T, IDX_DIM), lambda i: (0, 0), pipeline_mode=pl.Buffered(1))],
        out_specs=pl.BlockSpec((tq // ts, nkb, ts, kb), lambda i: (i, 0, 0, 0)),
        out_shape=jax.ShapeDtypeStruct((T // ts, nkb, ts, kb), BF16),
        scratch_shapes=[pltpu.VMEM((nkb, tq, kb), I32)],
        compiler_params=_cparams(("parallel",)),
        name="dsa_indexer",
    )(qi_h, w_col, ki)


def _dsa_kernel(q_ref, bias_ref, kc_ref, wuv_ref, o_ref, s_ref, p_ref, a_ref, m_ref, l_ref, acc_ref, *, tq, scale):
    i = pl.program_id(0)
    H, tk = MLA_HEADS, DSA_TK
    R = H * tq
    per = tk // DSA_KB
    n_steps = ((i + 1) * tq + tk - 1) // tk
    c = scale * LOG2E
    q = q_ref[...].reshape(R, KV_LORA + QK_ROPE)

    def keys(j):
        return kc_ref[pl.ds(pl.multiple_of(j * tk, tk), tk), :]

    m_ref[...] = jnp.full_like(m_ref, NEG)
    l_ref[...] = jnp.zeros_like(l_ref)
    acc_ref[...] = jnp.zeros_like(acc_ref)
    p_ref[1] = jnp.zeros((R, tk), BF16)
    a_ref[1] = jnp.ones((R, LANES), F32)
    s_ref[0] = _dot_nt(q, keys(0))
    lane_tiles = [slice(t * LANES, (t + 1) * LANES) for t in range(tk // LANES)]

    def add_values(slot, j):
        pv = _dot(p_ref[slot], keys(j)[:, :KV_LORA])
        a = a_ref[slot]
        for t in range(KV_LORA // LANES):
            cols = slice(t * LANES, (t + 1) * LANES)
            acc_ref[:, cols] = a * acc_ref[:, cols] + pv[:, cols]

    def step(j, cur):
        s_ref[1 - cur] = _dot_nt(q, keys(jnp.minimum(j + 1, n_steps - 1)))
        add_values(1 - cur, jnp.maximum(j - 1, 0))
        bias = jnp.concatenate([bias_ref[j * per + b] for b in range(per)], axis=1).astype(F32)
        for h in range(H):
            r = slice(h * tq, (h + 1) * tq)
            s = s_ref[cur, r, :] + bias
            m_prev = m_ref[r, :]
            m_new = jnp.maximum(m_prev, jnp.max(s, axis=1, keepdims=True))
            alpha = jnp.exp2((m_prev - m_new) * c)
            p = [jnp.exp2((s[:, t] - m_new) * c) for t in lane_tiles]
            l_ref[r, :] = alpha * l_ref[r, :] + sum(p[1:], p[0])
            m_ref[r, :] = m_new
            a_ref[cur, r, :] = alpha
            for t, pt in zip(lane_tiles, p):
                p_ref[cur, r, t] = pt.astype(BF16)

    def parity_step(j, carry):
        pl.when(j % 2 == 0)(lambda: step(j, 0))
        pl.when(j % 2 == 1)(lambda: step(j, 1))
        return carry

    lax.fori_loop(0, n_steps, parity_step, 0)
    add_values((n_steps - 1) % 2, n_steps - 1)

    for h in range(H):
        r = slice(h * tq, (h + 1) * tq)
        l = jnp.sum(l_ref[r, :], axis=1, keepdims=True)
        o_lat = (acc_ref[r, :] / l).astype(BF16)
        o_ref[:, h * V_HEAD:(h + 1) * V_HEAD] = _dot(o_lat, wuv_ref[h]).astype(o_ref.dtype)


def _dsa_attention(q576, bias, kc, w_uv_h, T):
    tq = min(DSA_TQ, T)
    H = MLA_HEADS
    C = KV_LORA + QK_ROPE
    nkb = T // DSA_KB
    assert T % DSA_TK == 0
    return pl.pallas_call(
        functools.partial(_dsa_kernel, tq=tq, scale=(QK_NOPE + QK_ROPE) ** -0.5),
        grid=(T // tq,),
        in_specs=[pl.BlockSpec((H, tq, C), lambda i: (0, i, 0)),
                  pl.BlockSpec((None, nkb, tq, DSA_KB), lambda i: (i, 0, 0, 0)),
                  pl.BlockSpec((T, C), lambda i: (0, 0), pipeline_mode=pl.Buffered(1)),
                  pl.BlockSpec((H, KV_LORA, V_HEAD), lambda i: (0, 0, 0), pipeline_mode=pl.Buffered(1))],
        out_specs=pl.BlockSpec((tq, H * V_HEAD), lambda i: (i, 0)),
        out_shape=jax.ShapeDtypeStruct((T, H * V_HEAD), BF16),
        scratch_shapes=[pltpu.VMEM((2, H * tq, DSA_TK), F32), pltpu.VMEM((2, H * tq, DSA_TK), BF16),
                        pltpu.VMEM((2, H * tq, LANES), F32),
                        pltpu.VMEM((H * tq, LANES), F32), pltpu.VMEM((H * tq, LANES), F32),
                        pltpu.VMEM((H * tq, KV_LORA), F32)],
        compiler_params=_cparams(("parallel",)),
        name="dsa_attention",
    )(q576, bias, kc, w_uv_h)


def _even_mixer(xb, T, w_in, w_out, lq1, lk1, lq2, lk2, g_subln, layer_idx, tabs):
    full = tabs[0]
    rope = [(0, 2 * MOBA_W, full, _swap_full),
            (3 * MOBA_W, 3 * MOBA_W + 2 * DIFF_W, full, _swap_full)]
    h0 = _matmul(xb, w_in, tm=2048, tn=1024, tk=1024, out_dtype=BF16, rope=rope, name="l0_in_proj")
    o_moba = _moba_attention(h0, T)
    o_diff = _diff_attention(h0, T, lq1, lk1, lq2, lk2, g_subln, layer_idx)
    o = jnp.concatenate([o_moba, o_diff], axis=1)
    return _matmul(o, w_out, tm=2048, tn=1024, tk=1024, out_dtype=F32, name="l0_out_proj")


def _odd_mixer(xb, T, w_in, g_q, g_kv, w_qb, w_uk, w_uv, w_iq, g_kidx, b_kidx, w_out, tabs):
    _, pair, idx = tabs
    D = w_in.shape[0]
    H = MLA_HEADS
    o_kpe, o_kidx, o_w = Q_LORA + KV_LORA, Q_LORA + KV_LORA + QK_ROPE, Q_LORA + KV_LORA + QK_ROPE + IDX_DIM
    w_in_p = jnp.concatenate([w_in[:, :o_kpe], w_in[:, o_kidx:o_w], w_in[:, o_kpe:o_kidx], w_in[:, o_w:],
                              jnp.zeros((D, L1_IN_PAD - w_in.shape[1]), w_in.dtype)], axis=1)
    h1 = _matmul(xb, w_in_p, tm=2048, tn=L1_IN_PAD // 2, tk=512, out_dtype=F32, name="l1_in_proj")
    cq, kc, ki, wi_full = _l1_post(h1, g_q, g_kv, g_kidx, b_kidx, idx)
    w_qb3 = w_qb.reshape(Q_LORA, H, QK_NOPE + QK_ROPE)
    w_qb_p = jnp.concatenate([w_qb3[:, :, :QK_NOPE].reshape(Q_LORA, H * QK_NOPE),
                              w_qb3[:, :, QK_NOPE:].reshape(Q_LORA, H * QK_ROPE)], axis=1)
    n0 = H * QK_NOPE
    q = _matmul(cq, w_qb_p, tm=2048, tn=1024, tk=Q_LORA, out_dtype=BF16,
                rope=[(n0, n0 + H * QK_ROPE, pair, _swap_pair)], name="l1_q_proj")
    qi = _matmul(cq, w_iq, tm=2048, tn=1024, tk=Q_LORA, out_dtype=BF16,
                 rope=[(0, IDX_HEADS * IDX_DIM, idx, _swap_pair)], name="l1_qi_proj")
    q_pe_h = q[:, n0:].reshape(T, H, QK_ROPE).transpose(1, 0, 2)
    q576 = _q_absorb(q, w_uk, q_pe_h)
    qi_h = qi.reshape(T, IDX_HEADS, IDX_DIM).transpose(1, 0, 2)
    w_col = wi_full[:, QK_ROPE:QK_ROPE + IDX_HEADS].T[:, :, None]
    bias = _indexer_mask(qi_h, w_col, ki, T, min(IDX_TOPK_MAX, T // 4))
    w_uv_h = w_uv.reshape(KV_LORA, H, V_HEAD).transpose(1, 0, 2).astype(BF16)
    o = _dsa_attention(q576, bias, kc, w_uv_h, T)
    return _matmul(o, w_out, tm=2048, tn=1024, tk=1024, out_dtype=F32, name="l1_out_proj")


def kernel(x, l0_w_in, l0_w_out, l0_lam_q1, l0_lam_k1, l0_lam_q2, l0_lam_k2, l0_g_subln, l1_w_in, l1_g_q, l1_g_kv, l1_w_qb, l1_w_uk, l1_w_uv, l1_w_iq, l1_g_kidx, l1_b_kidx, l1_w_out, ln_g, ln_b, moe_w_router, moe_b_router, moe_w_gu, moe_b_gu, moe_w_down, moe_b_down):
    B, T, D = x.shape
    assert B == 1
    xn = x.reshape(T, D)
    xb = xn.astype(BF16)
    tabs = _rope_tables(T)
    for i in range(DEPTH):
        if i % 2 == 0:
            mix = _even_mixer(xb, T, l0_w_in, l0_w_out, l0_lam_q1, l0_lam_k1, l0_lam_q2, l0_lam_k2,
                              l0_g_subln, i, tabs)
        else:
            mix = _odd_mixer(xb, T, l1_w_in, l1_g_q, l1_g_kv, l1_w_qb, l1_w_uk, l1_w_uv, l1_w_iq,
                             l1_g_kidx, l1_b_kidx, l1_w_out, tabs)
        xn, xb, top_idx, gates = _mix_ln_route(xn, mix, ln_g[i, 0], ln_b[i, 0], moe_w_router[i], moe_b_router[i])
        xn, xb = _moe(xn, xb, top_idx, gates, moe_w_gu, moe_b_gu, moe_w_down, moe_b_down, i,
                      ln_g[i, 1], ln_b[i, 1])
    return xn.reshape(B, T, D)
```

```python
import functools
import math

import jax
import jax.numpy as jnp
from jax import lax
from jax.experimental import pallas as pl
from jax.experimental.pallas import tpu as pltpu

F32 = jnp.float32
BF16 = jnp.bfloat16
I32 = jnp.int32

HEAD_DIM = 128
ROPE_THETA = 10000.0
MOBA_HEADS = 16
MOBA_BLOCK = 256
MOBA_TOPK = 3
DIFF_HEADS = 8
MOBA_W = MOBA_HEADS * HEAD_DIM
DIFF_W = DIFF_HEADS * 2 * HEAD_DIM
MLA_HEADS = 32
Q_LORA = 1024
KV_LORA = 512
QK_NOPE = 128
QK_ROPE = 64
V_HEAD = 128
IDX_HEADS = 32
IDX_DIM = 128
IDX_TOPK_MAX = 256
N_EXPERTS = 32
TOP_K = 4
SWIGLU_LIMIT = 7.0
SWIGLU_ALPHA = 1.702
DEPTH = 2
DN_ALPHA = (2 * DEPTH) ** 0.25
LN_EPS = 1e-5

LANES = 128
VMEM_LIMIT_BYTES = 56 * 2**20

NEG = -1e30
LOG2E = 1.4426950408889634
KEY_NEG_INF = -2139095041


def _cparams(semantics, flags=None):
    return pltpu.CompilerParams(dimension_semantics=semantics, vmem_limit_bytes=VMEM_LIMIT_BYTES, flags=flags)


def _dot(a, b):
    return jnp.dot(a, b, preferred_element_type=F32)


def _dot_nt(a, b, precision=None):
    return lax.dot_general(a, b, (((1,), (1,)), ((), ())), preferred_element_type=F32, precision=precision)


def _rope_tables(T):
    pos = jnp.arange(T).astype(F32)[:, None]
    inv64 = ROPE_THETA ** (-jnp.arange(64, dtype=F32) / 64)
    a64 = pos * inv64[None, :]
    c64, s64 = jnp.cos(a64), jnp.sin(a64)
    inv32 = ROPE_THETA ** (-jnp.arange(32, dtype=F32) / 32)
    a32 = pos * inv32[None, :]
    c32, s32 = jnp.cos(a32), jnp.sin(a32)
    one, zero = jnp.ones_like(c32), jnp.zeros_like(c32)
    full = (jnp.concatenate([c64, c64], 1), jnp.concatenate([-s64, s64], 1))
    pair = (jnp.concatenate([c32, c32, c32, c32], 1), jnp.concatenate([-s32, s32, -s32, s32], 1))
    idx = (jnp.concatenate([c32, c32, one, one], 1), jnp.concatenate([-s32, s32, zero, zero], 1))
    return full, pair, idx


def _swap_full(z):
    return pltpu.roll(z, 64, axis=1)


def _swap_pair(z):
    lane = lax.broadcasted_iota(I32, z.shape, 1)
    return jnp.where((lane % 64) < 32, pltpu.roll(z, 96, axis=1), pltpu.roll(z, 32, axis=1))


def _rope128(z, c, s, swap):
    return z * c + swap(z) * s


def _mm_kernel(*refs, nk, tn, rope_ranges, swaps):
    n_tab = len(swaps)
    a_ref, b_ref = refs[0], refs[1]
    tab_refs = refs[2:2 + 2 * n_tab]
    o_ref, acc_ref = refs[2 + 2 * n_tab], refs[3 + 2 * n_tab]
    k = pl.program_id(2)

    @pl.when(k == 0)
    def _():
        acc_ref[...] = jnp.zeros_like(acc_ref)

    acc_ref[...] += _dot(a_ref[...].astype(BF16), b_ref[...].astype(BF16))

    @pl.when(k == nk - 1)
    def _():
        if not rope_ranges:
            o_ref[...] = acc_ref[...].astype(o_ref.dtype)
            return
        j = pl.program_id(1)
        plain = None
        for (lo, hi, t) in rope_ranges:
            hit = (j >= lo) & (j < hi)
            plain = hit if plain is None else (plain | hit)

            @pl.when(hit)
            def _(t=t):
                c = tab_refs[2 * t][...]
                s = tab_refs[2 * t + 1][...]
                for g in range(tn // LANES):
                    z = acc_ref[:, g * LANES:(g + 1) * LANES]
                    o_ref[:, g * LANES:(g + 1) * LANES] = _rope128(z, c, s, swaps[t]).astype(o_ref.dtype)

        @pl.when(jnp.logical_not(plain))
        def _():
            o_ref[...] = acc_ref[...].astype(o_ref.dtype)


def _matmul(a, b, *, tm, tn, tk, out_dtype, rope=None, name="mm"):
    M, K = a.shape
    _, N = b.shape
    tm, tn, tk = min(tm, M), min(tn, N), min(tk, K)
    assert M % tm == 0 and N % tn == 0 and K % tk == 0
    rope = rope or []
    ranges, tabs, swaps = [], [], []
    for t, (lo, hi, (c, s), swap) in enumerate(rope):
        assert lo % tn == 0 and hi % tn == 0 and tn % LANES == 0
        ranges.append((lo // tn, hi // tn, t))
        tabs += [c, s]
        swaps.append(swap)
    nk = K // tk
    in_specs = [pl.BlockSpec((tm, tk), lambda i, j, k: (i, k)),
                pl.BlockSpec((tk, tn), lambda i, j, k: (k, j))]
    in_specs += [pl.BlockSpec((tm, LANES), lambda i, j, k: (i, 0)) for _ in tabs]
    return pl.pallas_call(
        functools.partial(_mm_kernel, nk=nk, tn=tn, rope_ranges=tuple(ranges), swaps=tuple(swaps)),
        grid=(M // tm, N // tn, nk),
        in_specs=in_specs,
        out_specs=pl.BlockSpec((tm, tn), lambda i, j, k: (i, j)),
        out_shape=jax.ShapeDtypeStruct((M, N), out_dtype),
        scratch_shapes=[pltpu.VMEM((tm, tn), F32)],
        compiler_params=_cparams(("parallel", "parallel", "arbitrary")),
        name=name,
    )(a, b, *tabs)


SOFTMAX_ROWS = 64


def _softmax_step(s, v, m_ref, l_ref, acc_ref, scale):
    c = scale * LOG2E
    tq, tk = s.shape
    dv = v.shape[1]
    rows = [slice(r, r + SOFTMAX_ROWS) for r in range(0, tq, SOFTMAX_ROWS)]
    p_rows, alphas = [], []
    for r in rows:
        sc = s[r, :]
        m_prev = m_ref[r, :]
        m_new = jnp.maximum(m_prev, jnp.max(sc, axis=1, keepdims=True))
        alpha = jnp.exp2((m_prev - m_new) * c)
        p = [jnp.exp2((sc[:, t:t + LANES] - m_new) * c) for t in range(0, tk, LANES)]
        l_ref[r, :] = alpha * l_ref[r, :] + sum(p[1:], p[0])
        m_ref[r, :] = m_new
        p_rows.append(jnp.concatenate([pt.astype(BF16) for pt in p], axis=1))
        alphas.append(alpha)
    pv = _dot(jnp.concatenate(p_rows, axis=0), v)
    for r, alpha in zip(rows, alphas):
        for t in range(0, dv, LANES):
            acc_ref[r, t:t + LANES] = alpha * acc_ref[r, t:t + LANES] + pv[r, t:t + LANES]


def _row_sum(l_ref):
    return jnp.sum(l_ref[...], axis=1, keepdims=True)


def _causal_mask(s, row0, col0):
    row = row0 + lax.broadcasted_iota(I32, s.shape, 0)
    col = col0 + lax.broadcasted_iota(I32, s.shape, 1)
    return jnp.where(col <= row, s, NEG)


MOBA_TILE = 512


def _moba_kernel(q_ref, k_ref, v_ref, o_ref, kmean_ref, m_ref, l_ref, acc_ref, *, T, tq, scale):
    i = pl.program_id(1)
    nb = T // MOBA_BLOCK
    per = tq // MOBA_BLOCK

    @pl.when(i == 0)
    def _():
        blk = lax.broadcasted_iota(I32, (LANES, T), 0)
        pos = lax.broadcasted_iota(I32, (LANES, T), 1)
        avg = jnp.where(pos // MOBA_BLOCK == blk, 1.0 / MOBA_BLOCK, 0.0).astype(BF16)
        kmean_ref[...] = _dot(avg, k_ref[...])

    q = q_ref[...]
    gate = _dot_nt(q.astype(F32), kmean_ref[...], precision=lax.Precision.HIGHEST)
    lane = lax.broadcasted_iota(I32, (tq, LANES), 1)
    lane_f = lane.astype(F32)
    own = i * per + lax.broadcasted_iota(I32, (tq, LANES), 0) // MOBA_BLOCK
    past = lane < own
    g = jnp.where(past, gate, -jnp.inf)
    sel = jnp.zeros((tq, LANES), jnp.bool_)
    for _ in range(min(MOBA_TOPK, nb)):
        mx = jnp.max(g, axis=1, keepdims=True)
        first = jnp.min(jnp.where(g == mx, lane_f, float(LANES)), axis=1, keepdims=True)
        pick = lane_f == first
        sel = sel | pick
        g = jnp.where(pick, -jnp.inf, g)
    visible = (sel & past) | (lane == own)
    q_aug = jnp.concatenate([q, jnp.where(visible, 0.0, NEG).astype(BF16)], axis=1)

    m_ref[...] = jnp.full_like(m_ref, NEG)
    l_ref[...] = jnp.zeros_like(l_ref)
    acc_ref[...] = jnp.zeros_like(acc_ref)
    key_blk = lax.broadcasted_iota(I32, (tq, LANES), 0) // MOBA_BLOCK
    key_lane = lax.broadcasted_iota(I32, (tq, LANES), 1)

    def tile(j, causal):
        rows = pl.ds(pl.multiple_of(j * tq, tq), tq)
        k_aug = jnp.concatenate([k_ref[rows, :], (key_lane == j * per + key_blk).astype(BF16)], axis=1)
        s = _dot_nt(q_aug, k_aug)
        if causal:
            s = _causal_mask(s, 0, 0)
        _softmax_step(s, v_ref[rows, :], m_ref, l_ref, acc_ref, scale)

    def past_tile(j, carry):
        tile(j, False)
        return carry

    lax.fori_loop(0, i, past_tile, 0)
    tile(i, True)
    o_ref[...] = (acc_ref[...] / _row_sum(l_ref)).astype(o_ref.dtype)


def _moba_attention(h0, T):
    tq = min(MOBA_TILE, T)
    assert T % tq == 0 and tq % MOBA_BLOCK == 0 and T // MOBA_BLOCK <= LANES
    H = MOBA_HEADS
    return pl.pallas_call(
        functools.partial(_moba_kernel, T=T, tq=tq, scale=HEAD_DIM ** -0.5),
        grid=(H, T // tq),
        in_specs=[pl.BlockSpec((tq, HEAD_DIM), lambda h, i: (i, h)),
                  pl.BlockSpec((T, HEAD_DIM), lambda h, i: (0, H + h)),
                  pl.BlockSpec((T, HEAD_DIM), lambda h, i: (0, 2 * H + h))],
        out_specs=pl.BlockSpec((tq, HEAD_DIM), lambda h, i: (i, h)),
        out_shape=jax.ShapeDtypeStruct((T, MOBA_W), BF16),
        scratch_shapes=[pltpu.VMEM((LANES, HEAD_DIM), F32),
                        pltpu.VMEM((tq, LANES), F32), pltpu.VMEM((tq, LANES), F32),
                        pltpu.VMEM((tq, HEAD_DIM), F32)],
        compiler_params=_cparams(("parallel", "arbitrary")),
        name="moba_attention",
    )(h0, h0, h0)


DIFF_TQ = 512


def _diff_kernel(lq1_ref, lk1_ref, lq2_ref, lk2_ref, g_ref, q_ref, k_ref, v_ref, o_ref,
                 m_ref, l_ref, acc_ref, *, tq, scale, lam_init):
    i = pl.program_id(1)
    m_ref[...] = jnp.full_like(m_ref, NEG)
    l_ref[...] = jnp.zeros_like(l_ref)
    acc_ref[...] = jnp.zeros_like(acc_ref)
    q = q_ref[...]

    def block(j, masked):
        rows = pl.ds(pl.multiple_of(j * tq, tq), tq)
        kj = k_ref[rows, :]
        vj = v_ref[rows, :]
        cols = [slice(mp * HEAD_DIM, (mp + 1) * HEAD_DIM) for mp in range(2)]
        scores = [_dot_nt(q[:, c], kj[:, c]) for c in cols]
        for mp in range(2):
            s = _causal_mask(scores[mp], 0, 0) if masked else scores[mp]
            _softmax_step(s, vj, m_ref.at[mp], l_ref.at[mp], acc_ref.at[mp], scale)

    def past_block(j, carry):
        block(j, False)
        return carry

    lax.fori_loop(0, i, past_block, 0)
    block(i, True)

    lam = (jnp.exp(jnp.sum(lq1_ref[...] * lk1_ref[...], axis=1, keepdims=True))
           - jnp.exp(jnp.sum(lq2_ref[...] * lk2_ref[...], axis=1, keepdims=True)) + lam_init)
    o = acc_ref[0] / _row_sum(l_ref.at[0]) - lam * (acc_ref[1] / _row_sum(l_ref.at[1]))
    o = o * lax.rsqrt(jnp.mean(o * o, axis=1, keepdims=True) + 1e-5) * g_ref[...]
    o_ref[...] = (o * (1.0 - lam_init)).astype(o_ref.dtype)


def _diff_attention(h0, T, lq1, lk1, lq2, lk2, g_subln, layer_idx):
    tq = min(DIFF_TQ, T)
    assert T % tq == 0
    W = 2 * HEAD_DIM
    q0, k0, v0 = 3 * MOBA_W // W, (3 * MOBA_W + DIFF_W) // W, (3 * MOBA_W + 2 * DIFF_W) // W
    lam_init = 0.8 - 0.6 * math.exp(-0.3 * layer_idx)
    vec = lambda a: a.reshape(1, -1).astype(F32)
    small = lambda n: pl.BlockSpec((1, n), lambda h, i: (0, 0))
    return pl.pallas_call(
        functools.partial(_diff_kernel, tq=tq, scale=HEAD_DIM ** -0.5, lam_init=lam_init),
        grid=(DIFF_HEADS, T // tq),
        in_specs=[small(HEAD_DIM)] * 4 + [small(W),
                  pl.BlockSpec((tq, W), lambda h, i: (i, q0 + h)),
                  pl.BlockSpec((T, W), lambda h, i: (0, k0 + h)),
                  pl.BlockSpec((T, W), lambda h, i: (0, v0 + h))],
        out_specs=pl.BlockSpec((tq, W), lambda h, i: (i, h)),
        out_shape=jax.ShapeDtypeStruct((T, DIFF_W), BF16),
        scratch_shapes=[pltpu.VMEM((2, tq, LANES), F32), pltpu.VMEM((2, tq, LANES), F32),
                        pltpu.VMEM((2, tq, W), F32)],
        compiler_params=_cparams(("parallel", "arbitrary")),
        name="diff_attention",
    )(vec(lq1), vec(lk1), vec(lq2), vec(lk2), vec(g_subln), h0, h0, h0)


def _layernorm(v, g, b):
    mu = jnp.mean(v, axis=1, keepdims=True)
    d = v - mu
    var = jnp.mean(d * d, axis=1, keepdims=True)
    return d * lax.rsqrt(var + LN_EPS) * g + b


def _route(xn, wr_ref, br_ref, idx_ref, gate_ref):
    logits = jnp.dot(xn, wr_ref[...], preferred_element_type=F32,
                     precision=lax.Precision.HIGHEST) + br_ref[...]
    lane = lax.broadcasted_iota(I32, logits.shape, 1)
    lane_f = lane.astype(F32)
    idx_out = jnp.zeros(logits.shape, F32)
    val_out = jnp.full(logits.shape, NEG, F32)
    g = logits
    for r in range(TOP_K):
        mx = jnp.max(g, axis=1, keepdims=True)
        first = jnp.min(jnp.where(g == mx, lane_f, float(LANES)), axis=1, keepdims=True)
        idx_out = jnp.where(lane == r, first, idx_out)
        val_out = jnp.where(lane == r, mx, val_out)
        g = jnp.where(lane_f == first, -jnp.inf, g)
    e = jnp.exp(val_out - jnp.max(val_out, axis=1, keepdims=True))
    gate_ref[...] = e / jnp.sum(e, axis=1, keepdims=True)
    idx_ref[...] = idx_out.astype(I32)


def _mix_ln_kernel(x_ref, mix_ref, g_ref, b_ref, wr_ref, br_ref, xn_ref, xb_ref, idx_ref, gate_ref):
    xn = _layernorm(DN_ALPHA * x_ref[...] + mix_ref[...], g_ref[...], b_ref[...])
    xn_ref[...] = xn
    xb_ref[...] = xn.astype(BF16)
    _route(xn, wr_ref, br_ref, idx_ref, gate_ref)


def _mix_ln_route(x, mix, g, b, w_router, b_router, tm=256):
    T, D = x.shape
    E = w_router.shape[1]
    wr = jnp.pad(w_router.astype(F32), ((0, 0), (0, LANES - E)))
    br = jnp.pad(b_router.astype(F32).reshape(1, E), ((0, 0), (0, LANES - E)), constant_values=NEG)
    row = lambda n: pl.BlockSpec((tm, n), lambda i: (i, 0))
    const = lambda r, n: pl.BlockSpec((r, n), lambda i: (0, 0))
    return pl.pallas_call(
        _mix_ln_kernel,
        grid=(T // tm,),
        in_specs=[row(D), row(D), const(1, D), const(1, D), const(D, LANES), const(1, LANES)],
        out_specs=[row(D), row(D), row(LANES), row(LANES)],
        out_shape=[jax.ShapeDtypeStruct((T, D), F32), jax.ShapeDtypeStruct((T, D), BF16),
                   jax.ShapeDtypeStruct((T, LANES), I32), jax.ShapeDtypeStruct((T, LANES), F32)],
        compiler_params=_cparams(("parallel",)),
        name="mix_ln_route",
    )(x, mix, g.reshape(1, D).astype(F32), b.reshape(1, D).astype(F32), wr, br)


def _combine_ln_kernel(x_ref, y_ref, gate_ref, g_ref, b_ref, xn_ref, xb_ref):
    gates = gate_ref[...]
    f = jnp.zeros(x_ref.shape, F32)
    for r in range(TOP_K):
        f = f + gates[:, r:r + 1] * y_ref[r].astype(F32)
    xn = _layernorm(DN_ALPHA * x_ref[...] + f, g_ref[...], b_ref[...])
    xn_ref[...] = xn
    xb_ref[...] = xn.astype(BF16)


def _combine_ln(x, y4, gates, g, b, tm=256):
    T, D = x.shape
    row = lambda n: pl.BlockSpec((tm, n), lambda i: (i, 0))
    const = lambda r, n: pl.BlockSpec((r, n), lambda i: (0, 0))
    return pl.pallas_call(
        _combine_ln_kernel,
        grid=(T // tm,),
        in_specs=[row(D), pl.BlockSpec((TOP_K, tm, D), lambda i: (0, i, 0)), row(LANES), const(1, D), const(1, D)],
        out_specs=[row(D), row(D)],
        out_shape=[jax.ShapeDtypeStruct((T, D), F32), jax.ShapeDtypeStruct((T, D), BF16)],
        compiler_params=_cparams(("parallel",)),
        name="moe_combine_ln",
    )(x, y4, gates, g.reshape(1, D).astype(F32), b.reshape(1, D).astype(F32))


MOE_TM = 1152
MOE_FC = 256
MOE_DC = 1024
MOE_SPLIT = 16


def _expert_kernel(te_ref, na_ref, x_ref, wgu_ref, bgu_ref, wd_ref, bd_ref, sel_ref, *rest, ng, fc, base):
    y_ref, act_ref = rest[-2:]
    s = pl.program_id(1)
    live = base + pl.program_id(0) < na_ref[0]

    @pl.when(live & (s < ng))
    def _():
        hg = _dot(x_ref[...], wgu_ref[...].astype(BF16)) + bgu_ref[...]
        gate = jnp.minimum(hg, SWIGLU_LIMIT)
        up = jnp.clip(hg, -SWIGLU_LIMIT, SWIGLU_LIMIT)
        up_next = pltpu.roll(up, 2 * fc - 1, axis=1)
        act = (up_next + 1.0) * (gate * (1.0 / (1.0 + jnp.exp(-SWIGLU_ALPHA * gate))))
        act_ref[s] = _dot(act.astype(BF16), sel_ref[...]).astype(BF16)

    @pl.when(live & (s >= ng))
    def _():
        y = bd_ref[...] + _dot(act_ref[0], wd_ref[0:fc, :].astype(BF16))
        for c in range(1, ng):
            y = y + _dot(act_ref[c], wd_ref[c * fc:(c + 1) * fc, :].astype(BF16))
        y_ref[...] = y.astype(y_ref.dtype)

    @pl.when(jnp.logical_not(live) & (s >= ng))
    def _():
        y_ref[...] = jnp.zeros_like(y_ref)


def _experts(xs, tile_expert, n_active, w_gu, b_gu, w_down, b_down, layer, base, n_tiles, y_prev=None):
    rows, D = xs.shape
    _, E, _, F2 = w_gu.shape
    F = F2 // 2
    tm, fc, dc = MOE_TM, min(MOE_FC, F), min(MOE_DC, D)
    ng, nd = F // fc, D // dc
    nt = rows // tm
    even = (lax.broadcasted_iota(I32, (2 * fc, fc), 0) == 2 * lax.broadcasted_iota(I32, (2 * fc, fc), 1)).astype(BF16)

    def live(i, na):
        return jnp.clip(jnp.minimum(base + i, na[0] - 1), base, base + nt - 1)

    def g_chunk(i, s, na):
        return jnp.where(base + i < na[0], jnp.minimum(s, ng - 1), ng - 1)

    def d_chunk(i, s, na):
        return jnp.where(base + i < na[0], jnp.clip(s - ng, 0, nd - 1), nd - 1)

    in_specs = [
        pl.BlockSpec((tm, D), lambda i, s, te, na: (live(i, na) - base, 0), pipeline_mode=pl.Buffered(1)),
        pl.BlockSpec((None, None, D, 2 * fc), lambda i, s, te, na: (layer, te[live(i, na)], 0, g_chunk(i, s, na))),
        pl.BlockSpec((None, None, 1, 2 * fc), lambda i, s, te, na: (layer, te[live(i, na)], 0, g_chunk(i, s, na))),
        pl.BlockSpec((None, None, F, dc), lambda i, s, te, na: (layer, te[live(i, na)], 0, d_chunk(i, s, na))),
        pl.BlockSpec((None, None, 1, dc), lambda i, s, te, na: (layer, te[live(i, na)], 0, d_chunk(i, s, na))),
        pl.BlockSpec((2 * fc, fc), lambda i, s, te, na: (0, 0)),
    ]
    args = [tile_expert, n_active, xs, w_gu, b_gu.reshape(-1, E, 1, F2), w_down, b_down.reshape(-1, E, 1, D), even]
    aliases = {}
    if y_prev is not None:
        in_specs.append(pl.BlockSpec(memory_space=pl.ANY))
        aliases = {len(args): 0}
        args.append(y_prev)
    grid_spec = pltpu.PrefetchScalarGridSpec(
        num_scalar_prefetch=2,
        grid=(nt, ng + nd),
        in_specs=in_specs,
        out_specs=pl.BlockSpec((tm, dc), lambda i, s, te, na: (base + i, jnp.clip(s - ng, 0, nd - 1))),
        scratch_shapes=[pltpu.VMEM((ng, tm, fc), BF16)],
    )
    return pl.pallas_call(
        functools.partial(_expert_kernel, ng=ng, fc=fc, base=base),
        grid_spec=grid_spec,
        out_shape=jax.ShapeDtypeStruct((n_tiles * tm, D), BF16),
        input_output_aliases=aliases,
        compiler_params=_cparams(("arbitrary", "arbitrary")),
        name="moe_experts",
    )(*args)


def _moe(xn, xb, top_idx, gates, w_gu, b_gu, w_down, b_down, layer, ln_g, ln_b):
    T, D = xn.shape
    E = w_gu.shape[1]
    tm = MOE_TM
    e_flat = top_idx[:, :TOP_K].reshape(-1)
    onehot = (e_flat[:, None] == jnp.arange(E, dtype=I32)[None, :]).astype(I32)
    rank = jnp.take_along_axis(jnp.cumsum(onehot, axis=0), e_flat[:, None], axis=1)[:, 0] - 1
    counts = jnp.sum(onehot, axis=0)
    padded = ((counts + tm - 1) // tm) * tm
    ends = jnp.cumsum(padded)
    starts = ends - padded
    slot = starts[e_flat] + rank
    n_tiles = (T * TOP_K + E * (tm - 1)) // tm
    P = n_tiles * tm
    token_of_slot = (jnp.arange(P, dtype=I32) % T).at[slot].set(jnp.arange(T * TOP_K, dtype=I32) // TOP_K)
    tile_expert = jnp.minimum(jnp.searchsorted(ends, jnp.arange(n_tiles, dtype=I32) * tm, side="right"),
                              E - 1).astype(I32)
    n_active = (ends[-1:] // tm).astype(I32)

    split = min(MOE_SPLIT, n_tiles - 1)
    y = None
    for lo, hi in ((0, split), (split, n_tiles)):
        xs = jnp.take(xb, token_of_slot[lo * tm:hi * tm], axis=0, mode="clip")
        y = _experts(xs, tile_expert, n_active, w_gu, b_gu, w_down, b_down, layer, lo, n_tiles, y)
    slot_km = slot.reshape(T, TOP_K).T.reshape(-1)
    y4 = jnp.take(y, slot_km, axis=0, mode="clip").reshape(TOP_K, T, D)
    return _combine_ln(xn, y4, gates, ln_g, ln_b)


L1_IN_PAD = 1792


def _l1_post_kernel(h_ref, gq_ref, gkv_ref, gk_ref, bk_ref, c_ref, s_ref, cq_ref, kc_ref, ki_ref, wi_ref):
    def rms(v, g):
        return v * lax.rsqrt(jnp.mean(v * v, axis=1, keepdims=True) + 1e-6) * g

    cq_ref[...] = rms(h_ref[:, :Q_LORA], gq_ref[...]).astype(BF16)
    kc_ref[:, :KV_LORA] = rms(h_ref[:, Q_LORA:Q_LORA + KV_LORA], gkv_ref[...]).astype(BF16)
    c, s = c_ref[...], s_ref[...]
    o_idx = Q_LORA + KV_LORA
    ki = _layernorm(h_ref[:, o_idx:o_idx + IDX_DIM], gk_ref[...], bk_ref[...])
    ki_ref[...] = _rope128(ki, c, s, _swap_pair).astype(BF16)
    tail = h_ref[:, o_idx + IDX_DIM:]
    kc_ref[:, KV_LORA:] = _rope128(tail, c, s, _swap_pair)[:, :QK_ROPE].astype(BF16)
    wi_ref[...] = tail * (IDX_HEADS ** -0.5 * IDX_DIM ** -0.5)


def _l1_post(h1, g_q, g_kv, g_kidx, b_kidx, idx_tab, tm=256):
    T = h1.shape[0]
    row = lambda n: pl.BlockSpec((tm, n), lambda i: (i, 0))
    const = lambda n: pl.BlockSpec((1, n), lambda i: (0, 0))
    vec = lambda a: a.reshape(1, -1).astype(F32)
    return pl.pallas_call(
        _l1_post_kernel,
        grid=(T // tm,),
        in_specs=[row(L1_IN_PAD), const(Q_LORA), const(KV_LORA), const(IDX_DIM), const(IDX_DIM),
                  row(LANES), row(LANES)],
        out_specs=[row(Q_LORA), row(KV_LORA + QK_ROPE), row(IDX_DIM), row(LANES)],
        out_shape=[jax.ShapeDtypeStruct((T, Q_LORA), BF16), jax.ShapeDtypeStruct((T, KV_LORA + QK_ROPE), BF16),
                   jax.ShapeDtypeStruct((T, IDX_DIM), BF16), jax.ShapeDtypeStruct((T, LANES), F32)],
        compiler_params=_cparams(("parallel",)),
        name="l1_post",
    )(h1, vec(g_q), vec(g_kv), vec(g_kidx), vec(b_kidx), idx_tab[0], idx_tab[1])


def _qlat_kernel(qn_ref, wuk_ref, qpe_ref, o_ref):
    ql = _dot_nt(qn_ref[...], wuk_ref[...].astype(BF16))
    o_ref[:, :KV_LORA] = ql.astype(BF16)
    o_ref[:, KV_LORA:] = qpe_ref[...]


def _q_absorb(q_nope, w_uk, q_pe_h, tm=1024):
    T = q_nope.shape[0]
    tm = min(tm, T)
    H = MLA_HEADS
    return pl.pallas_call(
        _qlat_kernel,
        grid=(H, T // tm),
        in_specs=[pl.BlockSpec((tm, QK_NOPE), lambda h, i: (i, h)),
                  pl.BlockSpec((KV_LORA, QK_NOPE), lambda h, i: (0, h)),
                  pl.BlockSpec((None, tm, QK_ROPE), lambda h, i: (h, i, 0))],
        out_specs=pl.BlockSpec((None, tm, KV_LORA + QK_ROPE), lambda h, i: (h, i, 0)),
        out_shape=jax.ShapeDtypeStruct((H, T, KV_LORA + QK_ROPE), BF16),
        compiler_params=_cparams(("parallel", "parallel")),
        name="q_absorb",
    )(q_nope, w_uk, q_pe_h)


DSA_TQ = 64
DSA_KB = 256
DSA_TK = 512
IDX_TQ = 128


def _indexer_kernel(qi_ref, w_ref, ki_ref, bias_ref, key_ref, *, T, tq, ts, n_sel):
    i = pl.program_id(0)
    kb = DSA_KB
    nkb = T // kb
    n_live = ((i + 1) * tq + kb - 1) // kb
    q = qi_ref[...].reshape(IDX_HEADS * tq, IDX_DIM)
    w = w_ref[...]

    def score_block(j, carry):
        rows = pl.ds(pl.multiple_of(j * kb, kb), kb)
        s = _dot_nt(q, ki_ref[rows, :])
        isc = jnp.sum(jnp.maximum(s, 0.0).reshape(IDX_HEADS, tq, kb) * w, axis=0) + 0.0
        qpos = i * tq + lax.broadcasted_iota(I32, (tq, kb), 0)
        kpos = j * kb + lax.broadcasted_iota(I32, (tq, kb), 1)
        isc = jnp.where(kpos <= qpos, isc, -jnp.inf)
        bits = pltpu.bitcast(isc, I32)
        key_ref[j] = jnp.where(bits < 0, bits ^ 0x7FFFFFFF, bits)
        return carry

    lax.fori_loop(0, n_live, score_block, 0)

    parts = [slice(r, r + ts) for r in range(0, tq, ts)]

    def count_ge(cand):
        cbs = [jnp.broadcast_to(cand[r], (ts, LANES)) for r in parts]

        def body(j, cs):
            out = []
            for r, cb, c in zip(parts, cbs, cs):
                blk = key_ref[j, r, :]
                for g in range(kb // LANES):
                    c = c + (blk[:, g * LANES:(g + 1) * LANES] >= cb).astype(F32)
                out.append(c)
            return tuple(out)

        cs = lax.fori_loop(0, n_live, body, tuple(jnp.zeros((ts, LANES), F32) for _ in parts))
        return jnp.concatenate([jnp.sum(c, axis=1, keepdims=True) for c in cs], axis=0)

    thr = jnp.full((tq, 1), -2**31, I32)
    for bit in range(31, -1, -1):
        cand = jnp.zeros((tq, 1), I32) if bit == 31 else thr + (1 << bit)
        thr = jnp.where(count_ge(cand) >= n_sel, cand, thr)

    def write_block(j, carry):
        blk = key_ref[j]
        mask = jnp.where((blk >= thr) & (blk > KEY_NEG_INF), 0.0, NEG).astype(BF16)
        for t, r in enumerate(parts):
            bias_ref[t, j] = mask[r]
        return carry

    lax.fori_loop(0, n_live, write_block, 0)

    def dead_block(j, carry):
        for t in range(len(parts)):
            bias_ref[t, j] = jnp.full((ts, kb), NEG, BF16)
        return carry

    lax.fori_loop(n_live, nkb, dead_block, 0)


def _indexer_mask(qi_h, w_col, ki, T, n_sel):
    tq, ts, kb = min(IDX_TQ, T), min(DSA_TQ, T), DSA_KB
    nkb = T // kb
    assert tq % ts == 0 and T % tq == 0
    return pl.pallas_call(
        functools.partial(_indexer_kernel, T=T, tq=tq, ts=ts, n_sel=n_sel),
        grid=(T // tq,),
        in_specs=[pl.BlockSpec((IDX_HEADS, tq, IDX_DIM), lambda i: (0, i, 0)),
                  pl.BlockSpec((IDX_HEADS, tq, 1), lambda i: (0, i, 0)),
                  pl.BlockSpec((T, IDX_DIM), lambda i: (0, 0), pipeline_mode=pl.Buffered(1))],
        out_specs=pl.BlockSpec((tq // ts, nkb, ts, kb), lambda i: (i, 0, 0, 0)),
        out_shape=jax.ShapeDtypeStruct((T // ts, nkb, ts, kb), BF16),
        scratch_shapes=[pltpu.VMEM((nkb, tq, kb), I32)],
        compiler_params=_cparams(("parallel",)),
        name="dsa_indexer",
    )(qi_h, w_col, ki)


def _dsa_kernel(q_ref, bias_ref, kc_ref, wuv_ref, o_ref, s_ref, p_ref, a_ref, m_ref, l_ref, acc_ref, *, tq, scale):
    i = pl.program_id(0)
    H, tk = MLA_HEADS, DSA_TK
    R = H * tq
    per = tk // DSA_KB
    n_steps = ((i + 1) * tq + tk - 1) // tk
    c = scale * LOG2E
    q = q_ref[...].reshape(R, KV_LORA + QK_ROPE)

    def keys(j):
        return kc_ref[pl.ds(pl.multiple_of(j * tk, tk), tk), :]

    m_ref[...] = jnp.full_like(m_ref, NEG)
    l_ref[...] = jnp.zeros_like(l_ref)
    acc_ref[...] = jnp.zeros_like(acc_ref)
    p_ref[1] = jnp.zeros((R, tk), BF16)
    a_ref[1] = jnp.ones((R, LANES), F32)
    s_ref[0] = _dot_nt(q, keys(0))
    lane_tiles = [slice(t * LANES, (t + 1) * LANES) for t in range(tk // LANES)]

    def add_values(slot, j):
        pv = _dot(p_ref[slot], keys(j)[:, :KV_LORA])
        a = a_ref[slot]
        for t in range(KV_LORA // LANES):
            cols = slice(t * LANES, (t + 1) * LANES)
            acc_ref[:, cols] = a * acc_ref[:, cols] + pv[:, cols]

    def step(j, cur):
        s_ref[1 - cur] = _dot_nt(q, keys(jnp.minimum(j + 1, n_steps - 1)))
        add_values(1 - cur, jnp.maximum(j - 1, 0))
        bias = jnp.concatenate([bias_ref[j * per + b] for b in range(per)], axis=1).astype(F32)
        for h in range(H):
            r = slice(h * tq, (h + 1) * tq)
            s = s_ref[cur, r, :] + bias
            m_prev = m_ref[r, :]
            m_new = jnp.maximum(m_prev, jnp.max(s, axis=1, keepdims=True))
            alpha = jnp.exp2((m_prev - m_new) * c)
            p = [jnp.exp2((s[:, t] - m_new) * c) for t in lane_tiles]
            l_ref[r, :] = alpha * l_ref[r, :] + sum(p[1:], p[0])
            m_ref[r, :] = m_new
            a_ref[cur, r, :] = alpha
            for t, pt in zip(lane_tiles, p):
                p_ref[cur, r, t] = pt.astype(BF16)

    def parity_step(j, carry):
        pl.when(j % 2 == 0)(lambda: step(j, 0))
        pl.when(j % 2 == 1)(lambda: step(j, 1))
        return carry

    lax.fori_loop(0, n_steps, parity_step, 0)
    add_values((n_steps - 1) % 2, n_steps - 1)

    for h in range(H):
        r = slice(h * tq, (h + 1) * tq)
        l = jnp.sum(l_ref[r, :], axis=1, keepdims=True)
        o_lat = (acc_ref[r, :] / l).astype(BF16)
        o_ref[:, h * V_HEAD:(h + 1) * V_HEAD] = _dot(o_lat, wuv_ref[h]).astype(o_ref.dtype)


def _dsa_attention(q576, bias, kc, w_uv_h, T):
    tq = min(DSA_TQ, T)
    H = MLA_HEADS
    C = KV_LORA + QK_ROPE
    nkb = T // DSA_KB
    assert T % DSA_TK == 0
    return pl.pallas_call(
        functools.partial(_dsa_kernel, tq=tq, scale=(QK_NOPE + QK_ROPE) ** -0.5),
        grid=(T // tq,),
        in_specs=[pl.BlockSpec((H, tq, C), lambda i: (0, i, 0)),
                  pl.BlockSpec((None, nkb, tq, DSA_KB), lambda i: (i, 0, 0, 0)),
                  pl.BlockSpec((T, C), lambda i: (0, 0), pipeline_mode=pl.Buffered(1)),
                  pl.BlockSpec((H, KV_LORA, V_HEAD), lambda i: (0, 0, 0), pipeline_mode=pl.Buffered(1))],
        out_specs=pl.BlockSpec((tq, H * V_HEAD), lambda i: (i, 0)),
        out_shape=jax.ShapeDtypeStruct((T, H * V_HEAD), BF16),
        scratch_shapes=[pltpu.VMEM((2, H * tq, DSA_TK), F32), pltpu.VMEM((2, H * tq, DSA_TK), BF16),
                        pltpu.VMEM((2, H * tq, LANES), F32),
                        pltpu.VMEM((H * tq, LANES), F32), pltpu.VMEM((H * tq, LANES), F32),
                        pltpu.VMEM((H * tq, KV_LORA), F32)],
        compiler_params=_cparams(("parallel",)),
        name="dsa_attention",
    )(q576, bias, kc, w_uv_h)


def _even_mixer(xb, T, w_in, w_out, lq1, lk1, lq2, lk2, g_subln, layer_idx, tabs):
    full = tabs[0]
    rope = [(0, 2 * MOBA_W, full, _swap_full),
            (3 * MOBA_W, 3 * MOBA_W + 2 * DIFF_W, full, _swap_full)]
    h0 = _matmul(xb, w_in, tm=2048, tn=1024, tk=1024, out_dtype=BF16, rope=rope, name="l0_in_proj")
    o_moba = _moba_attention(h0, T)
    o_diff = _diff_attention(h0, T, lq1, lk1, lq2, lk2, g_subln, layer_idx)
    o = jnp.concatenate([o_moba, o_diff], axis=1)
    return _matmul(o, w_out, tm=2048, tn=1024, tk=1024, out_dtype=F32, name="l0_out_proj")


def _odd_mixer(xb, T, w_in, g_q, g_kv, w_qb, w_uk, w_uv, w_iq, g_kidx, b_kidx, w_out, tabs):
    _, pair, idx = tabs
    D = w_in.shape[0]
    H = MLA_HEADS
    o_kpe, o_kidx, o_w = Q_LORA + KV_LORA, Q_LORA + KV_LORA + QK_ROPE, Q_LORA + KV_LORA + QK_ROPE + IDX_DIM
    w_in_p = jnp.concatenate([w_in[:, :o_kpe], w_in[:, o_kidx:o_w], w_in[:, o_kpe:o_kidx], w_in[:, o_w:],
                              jnp.zeros((D, L1_IN_PAD - w_in.shape[1]), w_in.dtype)], axis=1)
    h1 = _matmul(xb, w_in_p, tm=2048, tn=L1_IN_PAD // 2, tk=512, out_dtype=F32, name="l1_in_proj")
    cq, kc, ki, wi_full = _l1_post(h1, g_q, g_kv, g_kidx, b_kidx, idx)
    w_qb3 = w_qb.reshape(Q_LORA, H, QK_NOPE + QK_ROPE)
    w_qb_p = jnp.concatenate([w_qb3[:, :, :QK_NOPE].reshape(Q_LORA, H * QK_NOPE),
                              w_qb3[:, :, QK_NOPE:].reshape(Q_LORA, H * QK_ROPE)], axis=1)
    n0 = H * QK_NOPE
    q = _matmul(cq, w_qb_p, tm=2048, tn=1024, tk=Q_LORA, out_dtype=BF16,
                rope=[(n0, n0 + H * QK_ROPE, pair, _swap_pair)], name="l1_q_proj")
    qi = _matmul(cq, w_iq, tm=2048, tn=1024, tk=Q_LORA, out_dtype=BF16,
                 rope=[(0, IDX_HEADS * IDX_DIM, idx, _swap_pair)], name="l1_qi_proj")
    q_pe_h = q[:, n0:].reshape(T, H, QK_ROPE).transpose(1, 0, 2)
    q576 = _q_absorb(q, w_uk, q_pe_h)
    qi_h = qi.reshape(T, IDX_HEADS, IDX_DIM).transpose(1, 0, 2)
    w_col = wi_full[:, QK_ROPE:QK_ROPE + IDX_HEADS].T[:, :, None]
    bias = _indexer_mask(qi_h, w_col, ki, T, min(IDX_TOPK_MAX, T // 4))
    w_uv_h = w_uv.reshape(KV_LORA, H, V_HEAD).transpose(1, 0, 2).astype(BF16)
    o = _dsa_attention(q576, bias, kc, w_uv_h, T)
    return _matmul(o, w_out, tm=2048, tn=1024, tk=1024, out_dtype=F32, name="l1_out_proj")


def kernel(x, l0_w_in, l0_w_out, l0_lam_q1, l0_lam_k1, l0_lam_q2, l0_lam_k2, l0_g_subln, l1_w_in, l1_g_q, l1_g_kv, l1_w_qb, l1_w_uk, l1_w_uv, l1_w_iq, l1_g_kidx, l1_b_kidx, l1_w_out, ln_g, ln_b, moe_w_router, moe_b_router, moe_w_gu, moe_b_gu, moe_w_down, moe_b_down):
    B, T, D = x.shape
    assert B == 1
    xn = x.reshape(T, D)
    xb = xn.astype(BF16)
    tabs = _rope_tables(T)
    for i in range(DEPTH):
        if i % 2 == 0:
            mix = _even_mixer(xb, T, l0_w_in, l0_w_out, l0_lam_q1, l0_lam_k1, l0_lam_q2, l0_lam_k2,
                              l0_g_subln, i, tabs)
        else:
            mix = _odd_mixer(xb, T, l1_w_in, l1_g_q, l1_g_kv, l1_w_qb, l1_w_uk, l1_w_uv, l1_w_iq,
                             l1_g_kidx, l1_b_kidx, l1_w_out, tabs)
        xn, xb, top_idx, gates = _mix_ln_route(xn, mix, ln_g[i, 0], ln_b[i, 0], moe_w_router[i], moe_b_router[i])
        xn, xb = _moe(xn, xb, top_idx, gates, moe_w_gu, moe_b_gu, moe_w_down, moe_b_down, i,
                      ln_g[i, 1], ln_b[i, 1])
    return xn.reshape(B, T, D)
```

```python
import functools
import math

import jax
import jax.numpy as jnp
from jax import lax
from jax.experimental import pallas as pl
from jax.experimental.pallas import tpu as pltpu

F32 = jnp.float32
BF16 = jnp.bfloat16
I32 = jnp.int32

HEAD_DIM = 128
ROPE_THETA = 10000.0
MOBA_HEADS = 16
MOBA_BLOCK = 256
MOBA_TOPK = 3
DIFF_HEADS = 8
MOBA_W = MOBA_HEADS * HEAD_DIM
DIFF_W = DIFF_HEADS * 2 * HEAD_DIM
MLA_HEADS = 32
Q_LORA = 1024
KV_LORA = 512
QK_NOPE = 128
QK_ROPE = 64
V_HEAD = 128
IDX_HEADS = 32
IDX_DIM = 128
IDX_TOPK_MAX = 256
N_EXPERTS = 32
TOP_K = 4
SWIGLU_LIMIT = 7.0
SWIGLU_ALPHA = 1.702
DEPTH = 2
DN_ALPHA = (2 * DEPTH) ** 0.25
LN_EPS = 1e-5

LANES = 128
VMEM_LIMIT_BYTES = 56 * 2**20

NEG = -1e30
LOG2E = 1.4426950408889634
KEY_NEG_INF = -2139095041


def _cparams(semantics, flags=None):
    return pltpu.CompilerParams(dimension_semantics=semantics, vmem_limit_bytes=VMEM_LIMIT_BYTES, flags=flags)


def _dot(a, b):
    return jnp.dot(a, b, preferred_element_type=F32)


def _dot_nt(a, b, precision=None):
    return lax.dot_general(a, b, (((1,), (1,)), ((), ())), preferred_element_type=F32, precision=precision)


def _rope_tables(T):
    pos = jnp.arange(T).astype(F32)[:, None]
    inv64 = ROPE_THETA ** (-jnp.arange(64, dtype=F32) / 64)
    a64 = pos * inv64[None, :]
    c64, s64 = jnp.cos(a64), jnp.sin(a64)
    inv32 = ROPE_THETA ** (-jnp.arange(32, dtype=F32) / 32)
    a32 = pos * inv32[None, :]
    c32, s32 = jnp.cos(a32), jnp.sin(a32)
    one, zero = jnp.ones_like(c32), jnp.zeros_like(c32)
    full = (jnp.concatenate([c64, c64], 1), jnp.concatenate([-s64, s64], 1))
    pair = (jnp.concatenate([c32, c32, c32, c32], 1), jnp.concatenate([-s32, s32, -s32, s32], 1))
    idx = (jnp.concatenate([c32, c32, one, one], 1), jnp.concatenate([-s32, s32, zero, zero], 1))
    return full, pair, idx


def _swap_full(z):
    return pltpu.roll(z, 64, axis=1)


def _swap_pair(z):
    lane = lax.broadcasted_iota(I32, z.shape, 1)
    return jnp.where((lane % 64) < 32, pltpu.roll(z, 96, axis=1), pltpu.roll(z, 32, axis=1))


def _rope128(z, c, s, swap):
    return z * c + swap(z) * s


def _mm_kernel(*refs, nk, tn, rope_ranges, swaps, head_major):
    n_tab = len(swaps)
    a_ref, b_ref = refs[0], refs[1]
    tab_refs = refs[2:2 + 2 * n_tab]
    o_ref, acc_ref = refs[2 + 2 * n_tab], refs[3 + 2 * n_tab]
    k = pl.program_id(2)

    @pl.when(k == 0)
    def _():
        acc_ref[...] = jnp.zeros_like(acc_ref)

    acc_ref[...] += _dot(a_ref[...].astype(BF16), b_ref[...].astype(BF16))

    def store(g, val):
        if head_major:
            o_ref[g] = val.astype(o_ref.dtype)
        else:
            o_ref[:, g * LANES:(g + 1) * LANES] = val.astype(o_ref.dtype)

    def store_plain():
        if head_major:
            for g in range(tn // LANES):
                store(g, acc_ref[:, g * LANES:(g + 1) * LANES])
        else:
            o_ref[...] = acc_ref[...].astype(o_ref.dtype)

    @pl.when(k == nk - 1)
    def _():
        if not rope_ranges:
            store_plain()
            return
        j = pl.program_id(1)
        plain = None
        for (lo, hi, t) in rope_ranges:
            hit = (j >= lo) & (j < hi)
            plain = hit if plain is None else (plain | hit)

            @pl.when(hit)
            def _(t=t):
                c = tab_refs[2 * t][...]
                s = tab_refs[2 * t + 1][...]
                for g in range(tn // LANES):
                    store(g, _rope128(acc_ref[:, g * LANES:(g + 1) * LANES], c, s, swaps[t]))

        pl.when(jnp.logical_not(plain))(store_plain)


def _matmul(a, b, *, tm, tn, tk, out_dtype, rope=None, head_major=False, name="mm"):
    M, K = a.shape
    _, N = b.shape
    tm, tn, tk = min(tm, M), min(tn, N), min(tk, K)
    assert M % tm == 0 and N % tn == 0 and K % tk == 0
    rope = rope or []
    ranges, tabs, swaps = [], [], []
    for t, (lo, hi, (c, s), swap) in enumerate(rope):
        assert lo % tn == 0 and hi % tn == 0 and tn % LANES == 0
        ranges.append((lo // tn, hi // tn, t))
        tabs += [c, s]
        swaps.append(swap)
    nk = K // tk
    in_specs = [pl.BlockSpec((tm, tk), lambda i, j, k: (i, k)),
                pl.BlockSpec((tk, tn), lambda i, j, k: (k, j))]
    in_specs += [pl.BlockSpec((tm, LANES), lambda i, j, k: (i, 0)) for _ in tabs]
    if head_major:
        assert tn % LANES == 0
        out_spec = pl.BlockSpec((tn // LANES, tm, LANES), lambda i, j, k: (j, i, 0))
        out_shape = jax.ShapeDtypeStruct((N // LANES, M, LANES), out_dtype)
    else:
        out_spec = pl.BlockSpec((tm, tn), lambda i, j, k: (i, j))
        out_shape = jax.ShapeDtypeStruct((M, N), out_dtype)
    return pl.pallas_call(
        functools.partial(_mm_kernel, nk=nk, tn=tn, rope_ranges=tuple(ranges), swaps=tuple(swaps),
                          head_major=head_major),
        grid=(M // tm, N // tn, nk),
        in_specs=in_specs,
        out_specs=out_spec,
        out_shape=out_shape,
        scratch_shapes=[pltpu.VMEM((tm, tn), F32)],
        compiler_params=_cparams(("parallel", "parallel", "arbitrary")),
        name=name,
    )(a, b, *tabs)


SOFTMAX_ROWS = 64


def _softmax_step(s, v, m_ref, l_ref, acc_ref, scale):
    c = scale * LOG2E
    tq, tk = s.shape
    dv = v.shape[1]
    rows = [slice(r, r + SOFTMAX_ROWS) for r in range(0, tq, SOFTMAX_ROWS)]
    p_rows, alphas = [], []
    for r in rows:
        sc = s[r, :]
        m_prev = m_ref[r, :]
        m_new = jnp.maximum(m_prev, jnp.max(sc, axis=1, keepdims=True))
        alpha = jnp.exp2((m_prev - m_new) * c)
        p = [jnp.exp2((sc[:, t:t + LANES] - m_new) * c) for t in range(0, tk, LANES)]
        l_ref[r, :] = alpha * l_ref[r, :] + sum(p[1:], p[0])
        m_ref[r, :] = m_new
        p_rows.append(jnp.concatenate([pt.astype(BF16) for pt in p], axis=1))
        alphas.append(alpha)
    pv = _dot(jnp.concatenate(p_rows, axis=0), v)
    for r, alpha in zip(rows, alphas):
        for t in range(0, dv, LANES):
            acc_ref[r, t:t + LANES] = alpha * acc_ref[r, t:t + LANES] + pv[r, t:t + LANES]


def _row_sum(l_ref):
    return jnp.sum(l_ref[...], axis=1, keepdims=True)


def _causal_mask(s, row0, col0):
    row = row0 + lax.broadcasted_iota(I32, s.shape, 0)
    col = col0 + lax.broadcasted_iota(I32, s.shape, 1)
    return jnp.where(col <= row, s, NEG)


MOBA_TILE = 512


def _moba_kernel(q_ref, k_ref, v_ref, o_ref, kmean_ref, m_ref, l_ref, acc_ref, *, T, tq, scale):
    i = pl.program_id(1)
    nb = T // MOBA_BLOCK
    per = tq // MOBA_BLOCK

    @pl.when(i == 0)
    def _():
        blk = lax.broadcasted_iota(I32, (LANES, T), 0)
        pos = lax.broadcasted_iota(I32, (LANES, T), 1)
        avg = jnp.where(pos // MOBA_BLOCK == blk, 1.0 / MOBA_BLOCK, 0.0).astype(BF16)
        kmean_ref[...] = _dot(avg, k_ref[...])

    q = q_ref[...]
    gate = _dot_nt(q.astype(F32), kmean_ref[...], precision=lax.Precision.HIGHEST)
    lane = lax.broadcasted_iota(I32, (tq, LANES), 1)
    lane_f = lane.astype(F32)
    own = i * per + lax.broadcasted_iota(I32, (tq, LANES), 0) // MOBA_BLOCK
    past = lane < own
    g = jnp.where(past, gate, -jnp.inf)
    sel = jnp.zeros((tq, LANES), jnp.bool_)
    for _ in range(min(MOBA_TOPK, nb)):
        mx = jnp.max(g, axis=1, keepdims=True)
        first = jnp.min(jnp.where(g == mx, lane_f, float(LANES)), axis=1, keepdims=True)
        pick = lane_f == first
        sel = sel | pick
        g = jnp.where(pick, -jnp.inf, g)
    visible = (sel & past) | (lane == own)
    q_aug = jnp.concatenate([q, jnp.where(visible, 0.0, NEG).astype(BF16)], axis=1)

    m_ref[...] = jnp.full_like(m_ref, NEG)
    l_ref[...] = jnp.zeros_like(l_ref)
    acc_ref[...] = jnp.zeros_like(acc_ref)
    key_blk = lax.broadcasted_iota(I32, (tq, LANES), 0) // MOBA_BLOCK
    key_lane = lax.broadcasted_iota(I32, (tq, LANES), 1)

    def tile(j, causal):
        rows = pl.ds(pl.multiple_of(j * tq, tq), tq)
        k_aug = jnp.concatenate([k_ref[rows, :], (key_lane == j * per + key_blk).astype(BF16)], axis=1)
        s = _dot_nt(q_aug, k_aug)
        if causal:
            s = _causal_mask(s, 0, 0)
        _softmax_step(s, v_ref[rows, :], m_ref, l_ref, acc_ref, scale)

    def past_tile(j, carry):
        tile(j, False)
        return carry

    lax.fori_loop(0, i, past_tile, 0)
    tile(i, True)
    o_ref[...] = (acc_ref[...] / _row_sum(l_ref)).astype(o_ref.dtype)


def _moba_attention(h0, T):
    tq = min(MOBA_TILE, T)
    assert T % tq == 0 and tq % MOBA_BLOCK == 0 and T // MOBA_BLOCK <= LANES
    H = MOBA_HEADS
    return pl.pallas_call(
        functools.partial(_moba_kernel, T=T, tq=tq, scale=HEAD_DIM ** -0.5),
        grid=(H, T // tq),
        in_specs=[pl.BlockSpec((tq, HEAD_DIM), lambda h, i: (i, h)),
                  pl.BlockSpec((T, HEAD_DIM), lambda h, i: (0, H + h)),
                  pl.BlockSpec((T, HEAD_DIM), lambda h, i: (0, 2 * H + h))],
        out_specs=pl.BlockSpec((tq, HEAD_DIM), lambda h, i: (i, h)),
        out_shape=jax.ShapeDtypeStruct((T, MOBA_W), BF16),
        scratch_shapes=[pltpu.VMEM((LANES, HEAD_DIM), F32),
                        pltpu.VMEM((tq, LANES), F32), pltpu.VMEM((tq, LANES), F32),
                        pltpu.VMEM((tq, HEAD_DIM), F32)],
        compiler_params=_cparams(("parallel", "arbitrary")),
        name="moba_attention",
    )(h0, h0, h0)


DIFF_TQ = 512


def _diff_kernel(lq1_ref, lk1_ref, lq2_ref, lk2_ref, g_ref, q_ref, k_ref, v_ref, o_ref,
                 m_ref, l_ref, acc_ref, *, tq, scale, lam_init):
    i = pl.program_id(1)
    m_ref[...] = jnp.full_like(m_ref, NEG)
    l_ref[...] = jnp.zeros_like(l_ref)
    acc_ref[...] = jnp.zeros_like(acc_ref)
    q = q_ref[...]

    def block(j, masked):
        rows = pl.ds(pl.multiple_of(j * tq, tq), tq)
        kj = k_ref[rows, :]
        vj = v_ref[rows, :]
        cols = [slice(mp * HEAD_DIM, (mp + 1) * HEAD_DIM) for mp in range(2)]
        scores = [_dot_nt(q[:, c], kj[:, c]) for c in cols]
        for mp in range(2):
            s = _causal_mask(scores[mp], 0, 0) if masked else scores[mp]
            _softmax_step(s, vj, m_ref.at[mp], l_ref.at[mp], acc_ref.at[mp], scale)

    def past_block(j, carry):
        block(j, False)
        return carry

    lax.fori_loop(0, i, past_block, 0)
    block(i, True)

    lam = (jnp.exp(jnp.sum(lq1_ref[...] * lk1_ref[...], axis=1, keepdims=True))
           - jnp.exp(jnp.sum(lq2_ref[...] * lk2_ref[...], axis=1, keepdims=True)) + lam_init)
    o = acc_ref[0] / _row_sum(l_ref.at[0]) - lam * (acc_ref[1] / _row_sum(l_ref.at[1]))
    o = o * lax.rsqrt(jnp.mean(o * o, axis=1, keepdims=True) + 1e-5) * g_ref[...]
    o_ref[...] = (o * (1.0 - lam_init)).astype(o_ref.dtype)


def _diff_attention(h0, T, lq1, lk1, lq2, lk2, g_subln, layer_idx):
    tq = min(DIFF_TQ, T)
    assert T % tq == 0
    W = 2 * HEAD_DIM
    q0, k0, v0 = 3 * MOBA_W // W, (3 * MOBA_W + DIFF_W) // W, (3 * MOBA_W + 2 * DIFF_W) // W
    lam_init = 0.8 - 0.6 * math.exp(-0.3 * layer_idx)
    vec = lambda a: a.reshape(1, -1).astype(F32)
    small = lambda n: pl.BlockSpec((1, n), lambda h, i: (0, 0))
    return pl.pallas_call(
        functools.partial(_diff_kernel, tq=tq, scale=HEAD_DIM ** -0.5, lam_init=lam_init),
        grid=(DIFF_HEADS, T // tq),
        in_specs=[small(HEAD_DIM)] * 4 + [small(W),
                  pl.BlockSpec((tq, W), lambda h, i: (i, q0 + h)),
                  pl.BlockSpec((T, W), lambda h, i: (0, k0 + h)),
                  pl.BlockSpec((T, W), lambda h, i: (0, v0 + h))],
        out_specs=pl.BlockSpec((tq, W), lambda h, i: (i, h)),
        out_shape=jax.ShapeDtypeStruct((T, DIFF_W), BF16),
        scratch_shapes=[pltpu.VMEM((2, tq, LANES), F32), pltpu.VMEM((2, tq, LANES), F32),
                        pltpu.VMEM((2, tq, W), F32)],
        compiler_params=_cparams(("parallel", "arbitrary")),
        name="diff_attention",
    )(vec(lq1), vec(lk1), vec(lq2), vec(lk2), vec(g_subln), h0, h0, h0)


def _layernorm(v, g, b):
    mu = jnp.mean(v, axis=1, keepdims=True)
    d = v - mu
    var = jnp.mean(d * d, axis=1, keepdims=True)
    return d * lax.rsqrt(var + LN_EPS) * g + b


def _route(xn, wr_ref, br_ref, idx_ref, gate_ref):
    logits = jnp.dot(xn, wr_ref[...], preferred_element_type=F32,
                     precision=lax.Precision.HIGHEST) + br_ref[...]
    lane = lax.broadcasted_iota(I32, logits.shape, 1)
    lane_f = lane.astype(F32)
    idx_out = jnp.zeros(logits.shape, F32)
    val_out = jnp.full(logits.shape, NEG, F32)
    g = logits
    for r in range(TOP_K):
        mx = jnp.max(g, axis=1, keepdims=True)
        first = jnp.min(jnp.where(g == mx, lane_f, float(LANES)), axis=1, keepdims=True)
        idx_out = jnp.where(lane == r, first, idx_out)
        val_out = jnp.where(lane == r, mx, val_out)
        g = jnp.where(lane_f == first, -jnp.inf, g)
    e = jnp.exp(val_out - jnp.max(val_out, axis=1, keepdims=True))
    gate_ref[...] = e / jnp.sum(e, axis=1, keepdims=True)
    idx_ref[...] = idx_out.astype(I32)


def _mix_ln_kernel(x_ref, mix_ref, g_ref, b_ref, wr_ref, br_ref, xn_ref, xb_ref, idx_ref, gate_ref):
    xn = _layernorm(DN_ALPHA * x_ref[...] + mix_ref[...], g_ref[...], b_ref[...])
    xn_ref[...] = xn
    xb_ref[...] = xn.astype(BF16)
    _route(xn, wr_ref, br_ref, idx_ref, gate_ref)


def _mix_ln_route(x, mix, g, b, w_router, b_router, tm=256):
    T, D = x.shape
    E = w_router.shape[1]
    wr = jnp.pad(w_router.astype(F32), ((0, 0), (0, LANES - E)))
    br = jnp.pad(b_router.astype(F32).reshape(1, E), ((0, 0), (0, LANES - E)), constant_values=NEG)
    row = lambda n: pl.BlockSpec((tm, n), lambda i: (i, 0))
    const = lambda r, n: pl.BlockSpec((r, n), lambda i: (0, 0))
    return pl.pallas_call(
        _mix_ln_kernel,
        grid=(T // tm,),
        in_specs=[row(D), row(D), const(1, D), const(1, D), const(D, LANES), const(1, LANES)],
        out_specs=[row(D), row(D), row(LANES), row(LANES)],
        out_shape=[jax.ShapeDtypeStruct((T, D), F32), jax.ShapeDtypeStruct((T, D), BF16),
                   jax.ShapeDtypeStruct((T, LANES), I32), jax.ShapeDtypeStruct((T, LANES), F32)],
        compiler_params=_cparams(("parallel",)),
        name="mix_ln_route",
    )(x, mix, g.reshape(1, D).astype(F32), b.reshape(1, D).astype(F32), wr, br)


def _combine_ln_kernel(x_ref, y_ref, gate_ref, g_ref, b_ref, xn_ref, xb_ref):
    gates = gate_ref[...]
    f = jnp.zeros(x_ref.shape, F32)
    for r in range(TOP_K):
        f = f + gates[:, r:r + 1] * y_ref[r].astype(F32)
    xn = _layernorm(DN_ALPHA * x_ref[...] + f, g_ref[...], b_ref[...])
    xn_ref[...] = xn
    xb_ref[...] = xn.astype(BF16)


def _combine_ln(x, y4, gates, g, b, tm=256):
    T, D = x.shape
    row = lambda n: pl.BlockSpec((tm, n), lambda i: (i, 0))
    const = lambda r, n: pl.BlockSpec((r, n), lambda i: (0, 0))
    return pl.pallas_call(
        _combine_ln_kernel,
        grid=(T // tm,),
        in_specs=[row(D), pl.BlockSpec((TOP_K, tm, D), lambda i: (0, i, 0)), row(LANES), const(1, D), const(1, D)],
        out_specs=[row(D), row(D)],
        out_shape=[jax.ShapeDtypeStruct((T, D), F32), jax.ShapeDtypeStruct((T, D), BF16)],
        compiler_params=_cparams(("parallel",)),
        name="moe_combine_ln",
    )(x, y4, gates, g.reshape(1, D).astype(F32), b.reshape(1, D).astype(F32))


MOE_TM = 1152
MOE_FC = 256
MOE_DC = 1024
MOE_SPLIT = 16


def _expert_kernel(te_ref, na_ref, x_ref, wgu_ref, bgu_ref, wd_ref, bd_ref, sel_ref, *rest, ng, fc, base):
    y_ref, act_ref = rest[-2:]
    s = pl.program_id(1)
    live = base + pl.program_id(0) < na_ref[0]

    @pl.when(live & (s < ng))
    def _():
        hg = _dot(x_ref[...], wgu_ref[...].astype(BF16)) + bgu_ref[...]
        gate = jnp.minimum(hg, SWIGLU_LIMIT)
        up = jnp.clip(hg, -SWIGLU_LIMIT, SWIGLU_LIMIT)
        up_next = pltpu.roll(up, 2 * fc - 1, axis=1)
        act = (up_next + 1.0) * (gate * (1.0 / (1.0 + jnp.exp(-SWIGLU_ALPHA * gate))))
        act_ref[s] = _dot(act.astype(BF16), sel_ref[...]).astype(BF16)

    @pl.when(live & (s >= ng))
    def _():
        y = bd_ref[...] + _dot(act_ref[0], wd_ref[0:fc, :].astype(BF16))
        for c in range(1, ng):
            y = y + _dot(act_ref[c], wd_ref[c * fc:(c + 1) * fc, :].astype(BF16))
        y_ref[...] = y.astype(y_ref.dtype)

    @pl.when(jnp.logical_not(live) & (s >= ng))
    def _():
        y_ref[...] = jnp.zeros_like(y_ref)


def _experts(xs, tile_expert, n_active, w_gu, b_gu, w_down, b_down, layer, base, n_tiles, y_prev=None):
    rows, D = xs.shape
    _, E, _, F2 = w_gu.shape
    F = F2 // 2
    tm, fc, dc = MOE_TM, min(MOE_FC, F), min(MOE_DC, D)
    ng, nd = F // fc, D // dc
    nt = rows // tm
    even = (lax.broadcasted_iota(I32, (2 * fc, fc), 0) == 2 * lax.broadcasted_iota(I32, (2 * fc, fc), 1)).astype(BF16)

    def live(i, na):
        return jnp.clip(jnp.minimum(base + i, na[0] - 1), base, base + nt - 1)

    def g_chunk(i, s, na):
        return jnp.where(base + i < na[0], jnp.minimum(s, ng - 1), ng - 1)

    def d_chunk(i, s, na):
        return jnp.where(base + i < na[0], jnp.clip(s - ng, 0, nd - 1), nd - 1)

    in_specs = [
        pl.BlockSpec((tm, D), lambda i, s, te, na: (live(i, na) - base, 0), pipeline_mode=pl.Buffered(1)),
        pl.BlockSpec((None, None, D, 2 * fc), lambda i, s, te, na: (layer, te[live(i, na)], 0, g_chunk(i, s, na))),
        pl.BlockSpec((None, None, 1, 2 * fc), lambda i, s, te, na: (layer, te[live(i, na)], 0, g_chunk(i, s, na))),
        pl.BlockSpec((None, None, F, dc), lambda i, s, te, na: (layer, te[live(i, na)], 0, d_chunk(i, s, na))),
        pl.BlockSpec((None, None, 1, dc), lambda i, s, te, na: (layer, te[live(i, na)], 0, d_chunk(i, s, na))),
        pl.BlockSpec((2 * fc, fc), lambda i, s, te, na: (0, 0)),
    ]
    args = [tile_expert, n_active, xs, w_gu, b_gu.reshape(-1, E, 1, F2), w_down, b_down.reshape(-1, E, 1, D), even]
    aliases = {}
    if y_prev is not None:
        in_specs.append(pl.BlockSpec(memory_space=pl.ANY))
        aliases = {len(args): 0}
        args.append(y_prev)
    grid_spec = pltpu.PrefetchScalarGridSpec(
        num_scalar_prefetch=2,
        grid=(nt, ng + nd),
        in_specs=in_specs,
        out_specs=pl.BlockSpec((tm, dc), lambda i, s, te, na: (base + i, jnp.clip(s - ng, 0, nd - 1))),
        scratch_shapes=[pltpu.VMEM((ng, tm, fc), BF16)],
    )
    return pl.pallas_call(
        functools.partial(_expert_kernel, ng=ng, fc=fc, base=base),
        grid_spec=grid_spec,
        out_shape=jax.ShapeDtypeStruct((n_tiles * tm, D), BF16),
        input_output_aliases=aliases,
        compiler_params=_cparams(("arbitrary", "arbitrary")),
        name="moe_experts",
    )(*args)


def _moe(xn, xb, top_idx, gates, w_gu, b_gu, w_down, b_down, layer, ln_g, ln_b):
    T, D = xn.shape
    E = w_gu.shape[1]
    tm = MOE_TM
    e_flat = top_idx[:, :TOP_K].reshape(-1)
    onehot = (e_flat[:, None] == jnp.arange(E, dtype=I32)[None, :]).astype(I32)
    rank = jnp.take_along_axis(jnp.cumsum(onehot, axis=0), e_flat[:, None], axis=1)[:, 0] - 1
    counts = jnp.sum(onehot, axis=0)
    padded = ((counts + tm - 1) // tm) * tm
    ends = jnp.cumsum(padded)
    starts = ends - padded
    slot = starts[e_flat] + rank
    n_tiles = (T * TOP_K + E * (tm - 1)) // tm
    P = n_tiles * tm
    token_of_slot = (jnp.arange(P, dtype=I32) % T).at[slot].set(jnp.arange(T * TOP_K, dtype=I32) // TOP_K)
    tile_expert = jnp.minimum(jnp.searchsorted(ends, jnp.arange(n_tiles, dtype=I32) * tm, side="right"),
                              E - 1).astype(I32)
    n_active = (ends[-1:] // tm).astype(I32)

    split = min(MOE_SPLIT, n_tiles - 1)
    y = None
    for lo, hi in ((0, split), (split, n_tiles)):
        xs = jnp.take(xb, token_of_slot[lo * tm:hi * tm], axis=0, mode="clip")
        y = _experts(xs, tile_expert, n_active, w_gu, b_gu, w_down, b_down, layer, lo, n_tiles, y)
    slot_km = slot.reshape(T, TOP_K).T.reshape(-1)
    y4 = jnp.take(y, slot_km, axis=0, mode="clip").reshape(TOP_K, T, D)
    return _combine_ln(xn, y4, gates, ln_g, ln_b)


L1_IN_PAD = 1792


def _l1_post_kernel(h_ref, gq_ref, gkv_ref, gk_ref, bk_ref, c_ref, s_ref, cq_ref, kc_ref, ki_ref, wi_ref):
    def rms(v, g):
        return v * lax.rsqrt(jnp.mean(v * v, axis=1, keepdims=True) + 1e-6) * g

    cq_ref[...] = rms(h_ref[:, :Q_LORA], gq_ref[...]).astype(BF16)
    kc_ref[:, :KV_LORA] = rms(h_ref[:, Q_LORA:Q_LORA + KV_LORA], gkv_ref[...]).astype(BF16)
    c, s = c_ref[...], s_ref[...]
    o_idx = Q_LORA + KV_LORA
    ki = _layernorm(h_ref[:, o_idx:o_idx + IDX_DIM], gk_ref[...], bk_ref[...])
    ki_ref[...] = _rope128(ki, c, s, _swap_pair).astype(BF16)
    tail = h_ref[:, o_idx + IDX_DIM:]
    kc_ref[:, KV_LORA:] = _rope128(tail, c, s, _swap_pair)[:, :QK_ROPE].astype(BF16)
    wi_ref[...] = tail * (IDX_HEADS ** -0.5 * IDX_DIM ** -0.5)


def _l1_post(h1, g_q, g_kv, g_kidx, b_kidx, idx_tab, tm=256):
    T = h1.shape[0]
    row = lambda n: pl.BlockSpec((tm, n), lambda i: (i, 0))
    const = lambda n: pl.BlockSpec((1, n), lambda i: (0, 0))
    vec = lambda a: a.reshape(1, -1).astype(F32)
    return pl.pallas_call(
        _l1_post_kernel,
        grid=(T // tm,),
        in_specs=[row(L1_IN_PAD), const(Q_LORA), const(KV_LORA), const(IDX_DIM), const(IDX_DIM),
                  row(LANES), row(LANES)],
        out_specs=[row(Q_LORA), row(KV_LORA + QK_ROPE), row(IDX_DIM), row(LANES)],
        out_shape=[jax.ShapeDtypeStruct((T, Q_LORA), BF16), jax.ShapeDtypeStruct((T, KV_LORA + QK_ROPE), BF16),
                   jax.ShapeDtypeStruct((T, IDX_DIM), BF16), jax.ShapeDtypeStruct((T, LANES), F32)],
        compiler_params=_cparams(("parallel",)),
        name="l1_post",
    )(h1, vec(g_q), vec(g_kv), vec(g_kidx), vec(b_kidx), idx_tab[0], idx_tab[1])


def _qlat_kernel(qn_ref, wuk_ref, qpe_ref, o_ref):
    ql = _dot_nt(qn_ref[...], wuk_ref[...].astype(BF16))
    o_ref[:, :KV_LORA] = ql.astype(BF16)
    pe = qpe_ref[...]
    odd = pl.program_id(0) % 2 == 1

    @pl.when(jnp.logical_not(odd))
    def _():
        o_ref[:, KV_LORA:] = pe[:, :QK_ROPE]

    @pl.when(odd)
    def _():
        o_ref[:, KV_LORA:] = pe[:, QK_ROPE:]


def _q_absorb(q, w_uk, tm=1024):
    T = q.shape[0]
    tm = min(tm, T)
    H = MLA_HEADS
    pe0 = H * QK_NOPE // LANES
    return pl.pallas_call(
        _qlat_kernel,
        grid=(H, T // tm),
        in_specs=[pl.BlockSpec((tm, QK_NOPE), lambda h, i: (i, h)),
                  pl.BlockSpec((KV_LORA, QK_NOPE), lambda h, i: (0, h)),
                  pl.BlockSpec((tm, LANES), lambda h, i: (i, pe0 + h // 2))],
        out_specs=pl.BlockSpec((None, tm, KV_LORA + QK_ROPE), lambda h, i: (h, i, 0)),
        out_shape=jax.ShapeDtypeStruct((H, T, KV_LORA + QK_ROPE), BF16),
        compiler_params=_cparams(("parallel", "parallel")),
        name="q_absorb",
    )(q, w_uk, q)


DSA_TQ = 64
DSA_KB = 256
DSA_TK = 512
IDX_TQ = 128


def _indexer_kernel(qi_ref, w_ref, ki_ref, bias_ref, key_ref, *, T, tq, ts, n_sel):
    i = pl.program_id(0)
    kb = DSA_KB
    nkb = T // kb
    n_live = ((i + 1) * tq + kb - 1) // kb
    q = qi_ref[...].reshape(IDX_HEADS * tq, IDX_DIM)
    w = w_ref[...]

    def score_block(j, carry):
        rows = pl.ds(pl.multiple_of(j * kb, kb), kb)
        s = _dot_nt(q, ki_ref[rows, :])
        isc = jnp.sum(jnp.maximum(s, 0.0).reshape(IDX_HEADS, tq, kb) * w, axis=0) + 0.0
        qpos = i * tq + lax.broadcasted_iota(I32, (tq, kb), 0)
        kpos = j * kb + lax.broadcasted_iota(I32, (tq, kb), 1)
        isc = jnp.where(kpos <= qpos, isc, -jnp.inf)
        bits = pltpu.bitcast(isc, I32)
        key_ref[j] = jnp.where(bits < 0, bits ^ 0x7FFFFFFF, bits)
        return carry

    lax.fori_loop(0, n_live, score_block, 0)

    parts = [slice(r, r + ts) for r in range(0, tq, ts)]

    def count_ge(cand):
        cbs = [jnp.broadcast_to(cand[r], (ts, LANES)) for r in parts]

        def body(j, cs):
            out = []
            for r, cb, c in zip(parts, cbs, cs):
                blk = key_ref[j, r, :]
                for g in range(kb // LANES):
                    c = c + (blk[:, g * LANES:(g + 1) * LANES] >= cb).astype(F32)
                out.append(c)
            return tuple(out)

        cs = lax.fori_loop(0, n_live, body, tuple(jnp.zeros((ts, LANES), F32) for _ in parts))
        return jnp.concatenate([jnp.sum(c, axis=1, keepdims=True) for c in cs], axis=0)

    thr = jnp.full((tq, 1), -2**31, I32)
    for bit in range(31, -1, -1):
        cand = jnp.zeros((tq, 1), I32) if bit == 31 else thr + (1 << bit)
        thr = jnp.where(count_ge(cand) >= n_sel, cand, thr)

    def write_block(j, carry):
        blk = key_ref[j]
        mask = jnp.where((blk >= thr) & (blk > KEY_NEG_INF), 0.0, NEG).astype(BF16)
        for t, r in enumerate(parts):
            bias_ref[t, j] = mask[r]
        return carry

    lax.fori_loop(0, n_live, write_block, 0)

    def dead_block(j, carry):
        for t in range(len(parts)):
            bias_ref[t, j] = jnp.full((ts, kb), NEG, BF16)
        return carry

    lax.fori_loop(n_live, nkb, dead_block, 0)


def _indexer_mask(qi_h, w_col, ki, T, n_sel):
    tq, ts, kb = min(IDX_TQ, T), min(DSA_TQ, T), DSA_KB
    nkb = T // kb
    assert tq % ts == 0 and T % tq == 0
    return pl.pallas_call(
        functools.partial(_indexer_kernel, T=T, tq=tq, ts=ts, n_sel=n_sel),
        grid=(T // tq,),
        in_specs=[pl.BlockSpec((IDX_HEADS, tq, IDX_DIM), lambda i: (0, i, 0)),
                  pl.BlockSpec((IDX_HEADS, tq, 1), lambda i: (0, i, 0)),
                  pl.BlockSpec((T, IDX_DIM), lambda i: (0, 0), pipeline_mode=pl.Buffered(1))],
        out_specs=pl.BlockSpec((tq // ts, nkb, ts, kb), lambda i: (i, 0, 0, 0)),
        out_shape=jax.ShapeDtypeStruct((T // ts, nkb, ts, kb), BF16),
        scratch_shapes=[pltpu.VMEM((nkb, tq, kb), I32)],
        compiler_params=_cparams(("parallel",)),
        name="dsa_indexer",
    )(qi_h, w_col, ki)


def _dsa_kernel(q_ref, bias_ref, kc_ref, wuv_ref, o_ref, s_ref, p_ref, a_ref, m_ref, l_ref, acc_ref, *, tq, scale):
    i = pl.program_id(0)
    H, tk = MLA_HEADS, DSA_TK
    R = H * tq
    per = tk // DSA_KB
    n_steps = ((i + 1) * tq + tk - 1) // tk
    c = scale * LOG2E
    q = q_ref[...].reshape(R, KV_LORA + QK_ROPE)

    def keys(j):
        return kc_ref[pl.ds(pl.multiple_of(j * tk, tk), tk), :]

    m_ref[...] = jnp.full_like(m_ref, NEG)
    l_ref[...] = jnp.zeros_like(l_ref)
    acc_ref[...] = jnp.zeros_like(acc_ref)
    p_ref[1] = jnp.zeros((R, tk), BF16)
    a_ref[1] = jnp.ones((R, LANES), F32)
    s_ref[0] = _dot_nt(q, keys(0))
    lane_tiles = [slice(t * LANES, (t + 1) * LANES) for t in range(tk // LANES)]

    def add_values(slot, j):
        pv = _dot(p_ref[slot], keys(j)[:, :KV_LORA])
        a = a_ref[slot]
        for t in range(KV_LORA // LANES):
            cols = slice(t * LANES, (t + 1) * LANES)
            acc_ref[:, cols] = a * acc_ref[:, cols] + pv[:, cols]

    def step(j, cur):
        s_ref[1 - cur] = _dot_nt(q, keys(jnp.minimum(j + 1, n_steps - 1)))
        add_values(1 - cur, jnp.maximum(j - 1, 0))
        bias = jnp.concatenate([bias_ref[j * per + b] for b in range(per)], axis=1).astype(F32)
        for h in range(H):
            r = slice(h * tq, (h + 1) * tq)
            s = s_ref[cur, r, :] + bias
            m_prev = m_ref[r, :]
            m_new = jnp.maximum(m_prev, jnp.max(s, axis=1, keepdims=True))
            alpha = jnp.exp2((m_prev - m_new) * c)
            p = [jnp.exp2((s[:, t] - m_new) * c) for t in lane_tiles]
            l_ref[r, :] = alpha * l_ref[r, :] + sum(p[1:], p[0])
            m_ref[r, :] = m_new
            a_ref[cur, r, :] = alpha
            for t, pt in zip(lane_tiles, p):
                p_ref[cur, r, t] = pt.astype(BF16)

    def parity_step(j, carry):
        pl.when(j % 2 == 0)(lambda: step(j, 0))
        pl.when(j % 2 == 1)(lambda: step(j, 1))
        return carry

    lax.fori_loop(0, n_steps, parity_step, 0)
    add_values((n_steps - 1) % 2, n_steps - 1)

    for h in range(H):
        r = slice(h * tq, (h + 1) * tq)
        l = jnp.sum(l_ref[r, :], axis=1, keepdims=True)
        o_lat = (acc_ref[r, :] / l).astype(BF16)
        o_ref[:, h * V_HEAD:(h + 1) * V_HEAD] = _dot(o_lat, wuv_ref[h]).astype(o_ref.dtype)


def _dsa_attention(q576, bias, kc, w_uv_h, T):
    tq = min(DSA_TQ, T)
    H = MLA_HEADS
    C = KV_LORA + QK_ROPE
    nkb = T // DSA_KB
    assert T % DSA_TK == 0
    return pl.pallas_call(
        functools.partial(_dsa_kernel, tq=tq, scale=(QK_NOPE + QK_ROPE) ** -0.5),
        grid=(T // tq,),
        in_specs=[pl.BlockSpec((H, tq, C), lambda i: (0, i, 0)),
                  pl.BlockSpec((None, nkb, tq, DSA_KB), lambda i: (i, 0, 0, 0)),
                  pl.BlockSpec((T, C), lambda i: (0, 0), pipeline_mode=pl.Buffered(1)),
                  pl.BlockSpec((H, KV_LORA, V_HEAD), lambda i: (0, 0, 0), pipeline_mode=pl.Buffered(1))],
        out_specs=pl.BlockSpec((tq, H * V_HEAD), lambda i: (i, 0)),
        out_shape=jax.ShapeDtypeStruct((T, H * V_HEAD), BF16),
        scratch_shapes=[pltpu.VMEM((2, H * tq, DSA_TK), F32), pltpu.VMEM((2, H * tq, DSA_TK), BF16),
                        pltpu.VMEM((2, H * tq, LANES), F32),
                        pltpu.VMEM((H * tq, LANES), F32), pltpu.VMEM((H * tq, LANES), F32),
                        pltpu.VMEM((H * tq, KV_LORA), F32)],
        compiler_params=_cparams(("parallel",)),
        name="dsa_attention",
    )(q576, bias, kc, w_uv_h)


def _even_mixer(xb, T, w_in, w_out, lq1, lk1, lq2, lk2, g_subln, layer_idx, tabs):
    full = tabs[0]
    rope = [(0, 2 * MOBA_W, full, _swap_full),
            (3 * MOBA_W, 3 * MOBA_W + 2 * DIFF_W, full, _swap_full)]
    h0 = _matmul(xb, w_in, tm=2048, tn=1024, tk=1024, out_dtype=BF16, rope=rope, name="l0_in_proj")
    o_moba = _moba_attention(h0, T)
    o_diff = _diff_attention(h0, T, lq1, lk1, lq2, lk2, g_subln, layer_idx)
    o = jnp.concatenate([o_moba, o_diff], axis=1)
    return _matmul(o, w_out, tm=2048, tn=1024, tk=1024, out_dtype=F32, name="l0_out_proj")


def _odd_mixer(xb, T, w_in, g_q, g_kv, w_qb, w_uk, w_uv, w_iq, g_kidx, b_kidx, w_out, tabs):
    _, pair, idx = tabs
    D = w_in.shape[0]
    H = MLA_HEADS
    o_kpe, o_kidx, o_w = Q_LORA + KV_LORA, Q_LORA + KV_LORA + QK_ROPE, Q_LORA + KV_LORA + QK_ROPE + IDX_DIM
    w_in_p = jnp.concatenate([w_in[:, :o_kpe], w_in[:, o_kidx:o_w], w_in[:, o_kpe:o_kidx], w_in[:, o_w:],
                              jnp.zeros((D, L1_IN_PAD - w_in.shape[1]), w_in.dtype)], axis=1)
    h1 = _matmul(xb, w_in_p, tm=2048, tn=L1_IN_PAD // 2, tk=512, out_dtype=F32, name="l1_in_proj")
    cq, kc, ki, wi_full = _l1_post(h1, g_q, g_kv, g_kidx, b_kidx, idx)
    w_qb3 = w_qb.reshape(Q_LORA, H, QK_NOPE + QK_ROPE)
    w_qb_p = jnp.concatenate([w_qb3[:, :, :QK_NOPE].reshape(Q_LORA, H * QK_NOPE),
                              w_qb3[:, :, QK_NOPE:].reshape(Q_LORA, H * QK_ROPE)], axis=1)
    n0 = H * QK_NOPE
    q = _matmul(cq, w_qb_p, tm=2048, tn=1024, tk=Q_LORA, out_dtype=BF16,
                rope=[(n0, n0 + H * QK_ROPE, pair, _swap_pair)], name="l1_q_proj")
    qi_h = _matmul(cq, w_iq, tm=2048, tn=1024, tk=Q_LORA, out_dtype=BF16, head_major=True,
                   rope=[(0, IDX_HEADS * IDX_DIM, idx, _swap_pair)], name="l1_qi_proj")
    q576 = _q_absorb(q, w_uk)
    w_col = wi_full[:, QK_ROPE:QK_ROPE + IDX_HEADS].T[:, :, None]
    bias = _indexer_mask(qi_h, w_col, ki, T, min(IDX_TOPK_MAX, T // 4))
    w_uv_h = w_uv.reshape(KV_LORA, H, V_HEAD).transpose(1, 0, 2).astype(BF16)
    o = _dsa_attention(q576, bias, kc, w_uv_h, T)
    return _matmul(o, w_out, tm=2048, tn=1024, tk=1024, out_dtype=F32, name="l1_out_proj")


def kernel(x, l0_w_in, l0_w_out, l0_lam_q1, l0_lam_k1, l0_lam_q2, l0_lam_k2, l0_g_subln, l1_w_in, l1_g_q, l1_g_kv, l1_w_qb, l1_w_uk, l1_w_uv, l1_w_iq, l1_g_kidx, l1_b_kidx, l1_w_out, ln_g, ln_b, moe_w_router, moe_b_router, moe_w_gu, moe_b_gu, moe_w_down, moe_b_down):
    B, T, D = x.shape
    assert B == 1
    xn = x.reshape(T, D)
    xb = xn.astype(BF16)
    tabs = _rope_tables(T)
    for i in range(DEPTH):
        if i % 2 == 0:
            mix = _even_mixer(xb, T, l0_w_in, l0_w_out, l0_lam_q1, l0_lam_k1, l0_lam_q2, l0_lam_k2,
                              l0_g_subln, i, tabs)
        else:
            mix = _odd_mixer(xb, T, l1_w_in, l1_g_q, l1_g_kv, l1_w_qb, l1_w_uk, l1_w_uv, l1_w_iq,
                             l1_g_kidx, l1_b_kidx, l1_w_out, tabs)
        xn, xb, top_idx, gates = _mix_ln_route(xn, mix, ln_g[i, 0], ln_b[i, 0], moe_w_router[i], moe_b_router[i])
        xn, xb = _moe(xn, xb, top_idx, gates, moe_w_gu, moe_b_gu, moe_w_down, moe_b_down, i,
                      ln_g[i, 1], ln_b[i, 1])
    return xn.reshape(B, T, D)
```

```python
import functools
import math

import jax
import jax.numpy as jnp
from jax import lax
from jax.experimental import pallas as pl
from jax.experimental.pallas import tpu as pltpu

F32 = jnp.float32
BF16 = jnp.bfloat16
I32 = jnp.int32

HEAD_DIM = 128
ROPE_THETA = 10000.0
MOBA_HEADS = 16
MOBA_BLOCK = 256
MOBA_TOPK = 3
DIFF_HEADS = 8
MOBA_W = MOBA_HEADS * HEAD_DIM
DIFF_W = DIFF_HEADS * 2 * HEAD_DIM
MLA_HEADS = 32
Q_LORA = 1024
KV_LORA = 512
QK_NOPE = 128
QK_ROPE = 64
V_HEAD = 128
IDX_HEADS = 32
IDX_DIM = 128
IDX_TOPK_MAX = 256
N_EXPERTS = 32
TOP_K = 4
SWIGLU_LIMIT = 7.0
SWIGLU_ALPHA = 1.702
DEPTH = 2
DN_ALPHA = (2 * DEPTH) ** 0.25
LN_EPS = 1e-5

LANES = 128
VMEM_LIMIT_BYTES = 56 * 2**20

NEG = -1e30
LOG2E = 1.4426950408889634
KEY_NEG_INF = -2139095041


def _cparams(semantics, flags=None):
    return pltpu.CompilerParams(dimension_semantics=semantics, vmem_limit_bytes=VMEM_LIMIT_BYTES, flags=flags)


def _dot(a, b):
    return jnp.dot(a, b, preferred_element_type=F32)


def _dot_nt(a, b, precision=None):
    return lax.dot_general(a, b, (((1,), (1,)), ((), ())), preferred_element_type=F32, precision=precision)


def _rope_tables(T):
    pos = jnp.arange(T).astype(F32)[:, None]
    inv64 = ROPE_THETA ** (-jnp.arange(64, dtype=F32) / 64)
    a64 = pos * inv64[None, :]
    c64, s64 = jnp.cos(a64), jnp.sin(a64)
    inv32 = ROPE_THETA ** (-jnp.arange(32, dtype=F32) / 32)
    a32 = pos * inv32[None, :]
    c32, s32 = jnp.cos(a32), jnp.sin(a32)
    one, zero = jnp.ones_like(c32), jnp.zeros_like(c32)
    full = (jnp.concatenate([c64, c64], 1), jnp.concatenate([-s64, s64], 1))
    pair = (jnp.concatenate([c32, c32, c32, c32], 1), jnp.concatenate([-s32, s32, -s32, s32], 1))
    idx = (jnp.concatenate([c32, c32, one, one], 1), jnp.concatenate([-s32, s32, zero, zero], 1))
    return full, pair, idx


def _swap_full(z):
    return pltpu.roll(z, 64, axis=1)


def _swap_pair(z):
    lane = lax.broadcasted_iota(I32, z.shape, 1)
    return jnp.where((lane % 64) < 32, pltpu.roll(z, 96, axis=1), pltpu.roll(z, 32, axis=1))


def _rope128(z, c, s, swap):
    return z * c + swap(z) * s


def _mm_kernel(*refs, nk, tn, rope_ranges, swaps, head_major):
    n_tab = len(swaps)
    a_ref, b_ref = refs[0], refs[1]
    tab_refs = refs[2:2 + 2 * n_tab]
    o_ref, acc_ref = refs[2 + 2 * n_tab], refs[3 + 2 * n_tab]
    k = pl.program_id(2)

    @pl.when(k == 0)
    def _():
        acc_ref[...] = jnp.zeros_like(acc_ref)

    acc_ref[...] += _dot(a_ref[...].astype(BF16), b_ref[...].astype(BF16))

    def store(g, val):
        if head_major:
            o_ref[g] = val.astype(o_ref.dtype)
        else:
            o_ref[:, g * LANES:(g + 1) * LANES] = val.astype(o_ref.dtype)

    def store_plain():
        if head_major:
            for g in range(tn // LANES):
                store(g, acc_ref[:, g * LANES:(g + 1) * LANES])
        else:
            o_ref[...] = acc_ref[...].astype(o_ref.dtype)

    @pl.when(k == nk - 1)
    def _():
        if not rope_ranges:
            store_plain()
            return
        j = pl.program_id(1)
        plain = None
        for (lo, hi, t) in rope_ranges:
            hit = (j >= lo) & (j < hi)
            plain = hit if plain is None else (plain | hit)

            @pl.when(hit)
            def _(t=t):
                c = tab_refs[2 * t][...]
                s = tab_refs[2 * t + 1][...]
                for g in range(tn // LANES):
                    store(g, _rope128(acc_ref[:, g * LANES:(g + 1) * LANES], c, s, swaps[t]))

        pl.when(jnp.logical_not(plain))(store_plain)


def _matmul(a, b, *, tm, tn, tk, out_dtype, rope=None, head_major=False, name="mm"):
    M, K = a.shape
    _, N = b.shape
    tm, tn, tk = min(tm, M), min(tn, N), min(tk, K)
    assert M % tm == 0 and N % tn == 0 and K % tk == 0
    rope = rope or []
    ranges, tabs, swaps = [], [], []
    for t, (lo, hi, (c, s), swap) in enumerate(rope):
        assert lo % tn == 0 and hi % tn == 0 and tn % LANES == 0
        ranges.append((lo // tn, hi // tn, t))
        tabs += [c, s]
        swaps.append(swap)
    nk = K // tk
    in_specs = [pl.BlockSpec((tm, tk), lambda i, j, k: (i, k)),
                pl.BlockSpec((tk, tn), lambda i, j, k: (k, j))]
    in_specs += [pl.BlockSpec((tm, LANES), lambda i, j, k: (i, 0)) for _ in tabs]
    if head_major:
        assert tn % LANES == 0
        out_spec = pl.BlockSpec((tn // LANES, tm, LANES), lambda i, j, k: (j, i, 0))
        out_shape = jax.ShapeDtypeStruct((N // LANES, M, LANES), out_dtype)
    else:
        out_spec = pl.BlockSpec((tm, tn), lambda i, j, k: (i, j))
        out_shape = jax.ShapeDtypeStruct((M, N), out_dtype)
    return pl.pallas_call(
        functools.partial(_mm_kernel, nk=nk, tn=tn, rope_ranges=tuple(ranges), swaps=tuple(swaps),
                          head_major=head_major),
        grid=(M // tm, N // tn, nk),
        in_specs=in_specs,
        out_specs=out_spec,
        out_shape=out_shape,
        scratch_shapes=[pltpu.VMEM((tm, tn), F32)],
        compiler_params=_cparams(("parallel", "parallel", "arbitrary")),
        name=name,
    )(a, b, *tabs)


SOFTMAX_ROWS = 64


def _softmax_step(s, v, m_ref, l_ref, acc_ref, scale):
    c = scale * LOG2E
    tq, tk = s.shape
    dv = v.shape[1]
    rows = [slice(r, r + SOFTMAX_ROWS) for r in range(0, tq, SOFTMAX_ROWS)]
    p_rows, alphas = [], []
    for r in rows:
        sc = s[r, :]
        m_prev = m_ref[r, :]
        m_new = jnp.maximum(m_prev, jnp.max(sc, axis=1, keepdims=True))
        alpha = jnp.exp2((m_prev - m_new) * c)
        p = [jnp.exp2((sc[:, t:t + LANES] - m_new) * c) for t in range(0, tk, LANES)]
        l_ref[r, :] = alpha * l_ref[r, :] + sum(p[1:], p[0])
        m_ref[r, :] = m_new
        p_rows.append(jnp.concatenate([pt.astype(BF16) for pt in p], axis=1))
        alphas.append(alpha)
    pv = _dot(jnp.concatenate(p_rows, axis=0), v)
    for r, alpha in zip(rows, alphas):
        for t in range(0, dv, LANES):
            acc_ref[r, t:t + LANES] = alpha * acc_ref[r, t:t + LANES] + pv[r, t:t + LANES]


def _row_sum(l_ref):
    return jnp.sum(l_ref[...], axis=1, keepdims=True)


def _causal_mask(s, row0, col0):
    row = row0 + lax.broadcasted_iota(I32, s.shape, 0)
    col = col0 + lax.broadcasted_iota(I32, s.shape, 1)
    return jnp.where(col <= row, s, NEG)


MOBA_TILE = 1024


def _moba_kernel(q_ref, k_ref, v_ref, o_ref, kmean_ref, m_ref, l_ref, acc_ref, *, T, tq, scale):
    i = pl.program_id(1)
    nb = T // MOBA_BLOCK
    per = tq // MOBA_BLOCK

    @pl.when(i == 0)
    def _():
        blk = lax.broadcasted_iota(I32, (LANES, T), 0)
        pos = lax.broadcasted_iota(I32, (LANES, T), 1)
        avg = jnp.where(pos // MOBA_BLOCK == blk, 1.0 / MOBA_BLOCK, 0.0).astype(BF16)
        kmean_ref[...] = _dot(avg, k_ref[...])

    q = q_ref[...]
    gate = _dot_nt(q.astype(F32), kmean_ref[...], precision=lax.Precision.HIGHEST)
    lane = lax.broadcasted_iota(I32, (tq, LANES), 1)
    lane_f = lane.astype(F32)
    own = i * per + lax.broadcasted_iota(I32, (tq, LANES), 0) // MOBA_BLOCK
    past = lane < own
    g = jnp.where(past, gate, -jnp.inf)
    sel = jnp.zeros((tq, LANES), jnp.bool_)
    for _ in range(min(MOBA_TOPK, nb)):
        mx = jnp.max(g, axis=1, keepdims=True)
        first = jnp.min(jnp.where(g == mx, lane_f, float(LANES)), axis=1, keepdims=True)
        pick = lane_f == first
        sel = sel | pick
        g = jnp.where(pick, -jnp.inf, g)
    visible = (sel & past) | (lane == own)
    q_aug = jnp.concatenate([q, jnp.where(visible, 0.0, NEG).astype(BF16)], axis=1)

    m_ref[...] = jnp.full_like(m_ref, NEG)
    l_ref[...] = jnp.zeros_like(l_ref)
    acc_ref[...] = jnp.zeros_like(acc_ref)
    key_blk = lax.broadcasted_iota(I32, (tq, LANES), 0) // MOBA_BLOCK
    key_lane = lax.broadcasted_iota(I32, (tq, LANES), 1)

    def tile(j, causal):
        rows = pl.ds(pl.multiple_of(j * tq, tq), tq)
        k_aug = jnp.concatenate([k_ref[rows, :], (key_lane == j * per + key_blk).astype(BF16)], axis=1)
        s = _dot_nt(q_aug, k_aug)
        if causal:
            s = _causal_mask(s, 0, 0)
        _softmax_step(s, v_ref[rows, :], m_ref, l_ref, acc_ref, scale)

    def past_tile(j, carry):
        tile(j, False)
        return carry

    lax.fori_loop(0, i, past_tile, 0)
    tile(i, True)
    o_ref[...] = (acc_ref[...] / _row_sum(l_ref)).astype(o_ref.dtype)


def _moba_attention(h0, T):
    tq = min(MOBA_TILE, T)
    assert T % tq == 0 and tq % MOBA_BLOCK == 0 and T // MOBA_BLOCK <= LANES
    H = MOBA_HEADS
    return pl.pallas_call(
        functools.partial(_moba_kernel, T=T, tq=tq, scale=HEAD_DIM ** -0.5),
        grid=(H, T // tq),
        in_specs=[pl.BlockSpec((tq, HEAD_DIM), lambda h, i: (i, h)),
                  pl.BlockSpec((T, HEAD_DIM), lambda h, i: (0, H + h)),
                  pl.BlockSpec((T, HEAD_DIM), lambda h, i: (0, 2 * H + h))],
        out_specs=pl.BlockSpec((tq, HEAD_DIM), lambda h, i: (i, h)),
        out_shape=jax.ShapeDtypeStruct((T, MOBA_W), BF16),
        scratch_shapes=[pltpu.VMEM((LANES, HEAD_DIM), F32),
                        pltpu.VMEM((tq, LANES), F32), pltpu.VMEM((tq, LANES), F32),
                        pltpu.VMEM((tq, HEAD_DIM), F32)],
        compiler_params=_cparams(("parallel", "arbitrary")),
        name="moba_attention",
    )(h0, h0, h0)


DIFF_TQ = 1024


def _diff_kernel(lq1_ref, lk1_ref, lq2_ref, lk2_ref, g_ref, q_ref, k_ref, v_ref, o_ref,
                 m_ref, l_ref, acc_ref, *, tq, scale, lam_init):
    i = pl.program_id(1)
    m_ref[...] = jnp.full_like(m_ref, NEG)
    l_ref[...] = jnp.zeros_like(l_ref)
    acc_ref[...] = jnp.zeros_like(acc_ref)
    q = q_ref[...]

    def block(j, masked):
        rows = pl.ds(pl.multiple_of(j * tq, tq), tq)
        kj = k_ref[rows, :]
        vj = v_ref[rows, :]
        cols = [slice(mp * HEAD_DIM, (mp + 1) * HEAD_DIM) for mp in range(2)]
        scores = [_dot_nt(q[:, c], kj[:, c]) for c in cols]
        for mp in range(2):
            s = _causal_mask(scores[mp], 0, 0) if masked else scores[mp]
            _softmax_step(s, vj, m_ref.at[mp], l_ref.at[mp], acc_ref.at[mp], scale)

    def past_block(j, carry):
        block(j, False)
        return carry

    lax.fori_loop(0, i, past_block, 0)
    block(i, True)

    lam = (jnp.exp(jnp.sum(lq1_ref[...] * lk1_ref[...], axis=1, keepdims=True))
           - jnp.exp(jnp.sum(lq2_ref[...] * lk2_ref[...], axis=1, keepdims=True)) + lam_init)
    o = acc_ref[0] / _row_sum(l_ref.at[0]) - lam * (acc_ref[1] / _row_sum(l_ref.at[1]))
    o = o * lax.rsqrt(jnp.mean(o * o, axis=1, keepdims=True) + 1e-5) * g_ref[...]
    o_ref[...] = (o * (1.0 - lam_init)).astype(o_ref.dtype)


def _diff_attention(h0, T, lq1, lk1, lq2, lk2, g_subln, layer_idx):
    tq = min(DIFF_TQ, T)
    assert T % tq == 0
    W = 2 * HEAD_DIM
    q0, k0, v0 = 3 * MOBA_W // W, (3 * MOBA_W + DIFF_W) // W, (3 * MOBA_W + 2 * DIFF_W) // W
    lam_init = 0.8 - 0.6 * math.exp(-0.3 * layer_idx)
    vec = lambda a: a.reshape(1, -1).astype(F32)
    small = lambda n: pl.BlockSpec((1, n), lambda h, i: (0, 0))
    return pl.pallas_call(
        functools.partial(_diff_kernel, tq=tq, scale=HEAD_DIM ** -0.5, lam_init=lam_init),
        grid=(DIFF_HEADS, T // tq),
        in_specs=[small(HEAD_DIM)] * 4 + [small(W),
                  pl.BlockSpec((tq, W), lambda h, i: (i, q0 + h)),
                  pl.BlockSpec((T, W), lambda h, i: (0, k0 + h)),
                  pl.BlockSpec((T, W), lambda h, i: (0, v0 + h))],
        out_specs=pl.BlockSpec((tq, W), lambda h, i: (i, h)),
        out_shape=jax.ShapeDtypeStruct((T, DIFF_W), BF16),
        scratch_shapes=[pltpu.VMEM((2, tq, LANES), F32), pltpu.VMEM((2, tq, LANES), F32),
                        pltpu.VMEM((2, tq, W), F32)],
        compiler_params=_cparams(("parallel", "arbitrary")),
        name="diff_attention",
    )(vec(lq1), vec(lk1), vec(lq2), vec(lk2), vec(g_subln), h0, h0, h0)


def _layernorm(v, g, b):
    mu = jnp.mean(v, axis=1, keepdims=True)
    d = v - mu
    var = jnp.mean(d * d, axis=1, keepdims=True)
    return d * lax.rsqrt(var + LN_EPS) * g + b


def _route(xn, wr_ref, br_ref, idx_ref, gate_ref):
    logits = jnp.dot(xn, wr_ref[...], preferred_element_type=F32,
                     precision=lax.Precision.HIGHEST) + br_ref[...]
    lane = lax.broadcasted_iota(I32, logits.shape, 1)
    lane_f = lane.astype(F32)
    idx_out = jnp.zeros(logits.shape, F32)
    val_out = jnp.full(logits.shape, NEG, F32)
    g = logits
    for r in range(TOP_K):
        mx = jnp.max(g, axis=1, keepdims=True)
        first = jnp.min(jnp.where(g == mx, lane_f, float(LANES)), axis=1, keepdims=True)
        idx_out = jnp.where(lane == r, first, idx_out)
        val_out = jnp.where(lane == r, mx, val_out)
        g = jnp.where(lane_f == first, -jnp.inf, g)
    e = jnp.exp(val_out - jnp.max(val_out, axis=1, keepdims=True))
    gate_ref[...] = e / jnp.sum(e, axis=1, keepdims=True)
    idx_ref[...] = idx_out.astype(I32)


def _mix_ln_kernel(x_ref, mix_ref, g_ref, b_ref, wr_ref, br_ref, xn_ref, xb_ref, idx_ref, gate_ref):
    xn = _layernorm(DN_ALPHA * x_ref[...] + mix_ref[...], g_ref[...], b_ref[...])
    xn_ref[...] = xn
    xb_ref[...] = xn.astype(BF16)
    _route(xn, wr_ref, br_ref, idx_ref, gate_ref)


def _mix_ln_route(x, mix, g, b, w_router, b_router, tm=256):
    T, D = x.shape
    E = w_router.shape[1]
    wr = jnp.pad(w_router.astype(F32), ((0, 0), (0, LANES - E)))
    br = jnp.pad(b_router.astype(F32).reshape(1, E), ((0, 0), (0, LANES - E)), constant_values=NEG)
    row = lambda n: pl.BlockSpec((tm, n), lambda i: (i, 0))
    const = lambda r, n: pl.BlockSpec((r, n), lambda i: (0, 0))
    return pl.pallas_call(
        _mix_ln_kernel,
        grid=(T // tm,),
        in_specs=[row(D), row(D), const(1, D), const(1, D), const(D, LANES), const(1, LANES)],
        out_specs=[row(D), row(D), row(LANES), row(LANES)],
        out_shape=[jax.ShapeDtypeStruct((T, D), F32), jax.ShapeDtypeStruct((T, D), BF16),
                   jax.ShapeDtypeStruct((T, LANES), I32), jax.ShapeDtypeStruct((T, LANES), F32)],
        compiler_params=_cparams(("parallel",)),
        name="mix_ln_route",
    )(x, mix, g.reshape(1, D).astype(F32), b.reshape(1, D).astype(F32), wr, br)


def _combine_ln_kernel(x_ref, y_ref, gate_ref, g_ref, b_ref, xn_ref, xb_ref):
    gates = gate_ref[...]
    f = jnp.zeros(x_ref.shape, F32)
    for r in range(TOP_K):
        f = f + gates[:, r:r + 1] * y_ref[r].astype(F32)
    xn = _layernorm(DN_ALPHA * x_ref[...] + f, g_ref[...], b_ref[...])
    xn_ref[...] = xn
    xb_ref[...] = xn.astype(BF16)


def _combine_ln(x, y4, gates, g, b, tm=256):
    T, D = x.shape
    row = lambda n: pl.BlockSpec((tm, n), lambda i: (i, 0))
    const = lambda r, n: pl.BlockSpec((r, n), lambda i: (0, 0))
    return pl.pallas_call(
        _combine_ln_kernel,
        grid=(T // tm,),
        in_specs=[row(D), pl.BlockSpec((TOP_K, tm, D), lambda i: (0, i, 0)), row(LANES), const(1, D), const(1, D)],
        out_specs=[row(D), row(D)],
        out_shape=[jax.ShapeDtypeStruct((T, D), F32), jax.ShapeDtypeStruct((T, D), BF16)],
        compiler_params=_cparams(("parallel",)),
        name="moe_combine_ln",
    )(x, y4, gates, g.reshape(1, D).astype(F32), b.reshape(1, D).astype(F32))


MOE_TM = 1152
MOE_FC = 256
MOE_DC = 1024
MOE_SPLIT = 16


def _expert_kernel(te_ref, na_ref, x_ref, wgu_ref, bgu_ref, wd_ref, bd_ref, sel_ref, *rest, ng, fc, base):
    y_ref, act_ref = rest[-2:]
    s = pl.program_id(1)
    live = base + pl.program_id(0) < na_ref[0]

    @pl.when(live & (s < ng))
    def _():
        hg = _dot(x_ref[...], wgu_ref[...].astype(BF16)) + bgu_ref[...]
        gate = jnp.minimum(hg, SWIGLU_LIMIT)
        up = jnp.clip(hg, -SWIGLU_LIMIT, SWIGLU_LIMIT)
        up_next = pltpu.roll(up, 2 * fc - 1, axis=1)
        act = (up_next + 1.0) * (gate * (1.0 / (1.0 + jnp.exp(-SWIGLU_ALPHA * gate))))
        act_ref[s] = _dot(act.astype(BF16), sel_ref[...]).astype(BF16)

    @pl.when(live & (s >= ng))
    def _():
        y = bd_ref[...] + _dot(act_ref[0], wd_ref[0:fc, :].astype(BF16))
        for c in range(1, ng):
            y = y + _dot(act_ref[c], wd_ref[c * fc:(c + 1) * fc, :].astype(BF16))
        y_ref[...] = y.astype(y_ref.dtype)

    @pl.when(jnp.logical_not(live) & (s >= ng))
    def _():
        y_ref[...] = jnp.zeros_like(y_ref)


def _experts(xs, tile_expert, n_active, w_gu, b_gu, w_down, b_down, layer, base, n_tiles, y_prev=None):
    rows, D = xs.shape
    _, E, _, F2 = w_gu.shape
    F = F2 // 2
    tm, fc, dc = MOE_TM, min(MOE_FC, F), min(MOE_DC, D)
    ng, nd = F // fc, D // dc
    nt = rows // tm
    even = (lax.broadcasted_iota(I32, (2 * fc, fc), 0) == 2 * lax.broadcasted_iota(I32, (2 * fc, fc), 1)).astype(BF16)

    def live(i, na):
        return jnp.clip(jnp.minimum(base + i, na[0] - 1), base, base + nt - 1)

    def g_chunk(i, s, na):
        return jnp.where(base + i < na[0], jnp.minimum(s, ng - 1), ng - 1)

    def d_chunk(i, s, na):
        return jnp.where(base + i < na[0], jnp.clip(s - ng, 0, nd - 1), nd - 1)

    in_specs = [
        pl.BlockSpec((tm, D), lambda i, s, te, na: (live(i, na) - base, 0), pipeline_mode=pl.Buffered(1)),
        pl.BlockSpec((None, None, D, 2 * fc), lambda i, s, te, na: (layer, te[live(i, na)], 0, g_chunk(i, s, na))),
        pl.BlockSpec((None, None, 1, 2 * fc), lambda i, s, te, na: (layer, te[live(i, na)], 0, g_chunk(i, s, na))),
        pl.BlockSpec((None, None, F, dc), lambda i, s, te, na: (layer, te[live(i, na)], 0, d_chunk(i, s, na))),
        pl.BlockSpec((None, None, 1, dc), lambda i, s, te, na: (layer, te[live(i, na)], 0, d_chunk(i, s, na))),
        pl.BlockSpec((2 * fc, fc), lambda i, s, te, na: (0, 0)),
    ]
    args = [tile_expert, n_active, xs, w_gu, b_gu.reshape(-1, E, 1, F2), w_down, b_down.reshape(-1, E, 1, D), even]
    aliases = {}
    if y_prev is not None:
        in_specs.append(pl.BlockSpec(memory_space=pl.ANY))
        aliases = {len(args): 0}
        args.append(y_prev)
    grid_spec = pltpu.PrefetchScalarGridSpec(
        num_scalar_prefetch=2,
        grid=(nt, ng + nd),
        in_specs=in_specs,
        out_specs=pl.BlockSpec((tm, dc), lambda i, s, te, na: (base + i, jnp.clip(s - ng, 0, nd - 1))),
        scratch_shapes=[pltpu.VMEM((ng, tm, fc), BF16)],
    )
    return pl.pallas_call(
        functools.partial(_expert_kernel, ng=ng, fc=fc, base=base),
        grid_spec=grid_spec,
        out_shape=jax.ShapeDtypeStruct((n_tiles * tm, D), BF16),
        input_output_aliases=aliases,
        compiler_params=_cparams(("arbitrary", "arbitrary")),
        name="moe_experts",
    )(*args)


def _moe(xn, xb, top_idx, gates, w_gu, b_gu, w_down, b_down, layer, ln_g, ln_b):
    T, D = xn.shape
    E = w_gu.shape[1]
    tm = MOE_TM
    e_flat = top_idx[:, :TOP_K].reshape(-1)
    onehot = (e_flat[:, None] == jnp.arange(E, dtype=I32)[None, :]).astype(I32)
    rank = jnp.take_along_axis(jnp.cumsum(onehot, axis=0), e_flat[:, None], axis=1)[:, 0] - 1
    counts = jnp.sum(onehot, axis=0)
    padded = ((counts + tm - 1) // tm) * tm
    ends = jnp.cumsum(padded)
    starts = ends - padded
    slot = starts[e_flat] + rank
    n_tiles = (T * TOP_K + E * (tm - 1)) // tm
    P = n_tiles * tm
    token_of_slot = (jnp.arange(P, dtype=I32) % T).at[slot].set(jnp.arange(T * TOP_K, dtype=I32) // TOP_K)
    tile_expert = jnp.minimum(jnp.searchsorted(ends, jnp.arange(n_tiles, dtype=I32) * tm, side="right"),
                              E - 1).astype(I32)
    n_active = (ends[-1:] // tm).astype(I32)

    split = min(MOE_SPLIT, n_tiles - 1)
    y = None
    for lo, hi in ((0, split), (split, n_tiles)):
        xs = jnp.take(xb, token_of_slot[lo * tm:hi * tm], axis=0, mode="clip")
        y = _experts(xs, tile_expert, n_active, w_gu, b_gu, w_down, b_down, layer, lo, n_tiles, y)
    slot_km = slot.reshape(T, TOP_K).T.reshape(-1)
    y4 = jnp.take(y, slot_km, axis=0, mode="clip").reshape(TOP_K, T, D)
    return _combine_ln(xn, y4, gates, ln_g, ln_b)


L1_IN_PAD = 1792


def _l1_post_kernel(h_ref, gq_ref, gkv_ref, gk_ref, bk_ref, c_ref, s_ref, cq_ref, kc_ref, ki_ref, wi_ref):
    def rms(v, g):
        return v * lax.rsqrt(jnp.mean(v * v, axis=1, keepdims=True) + 1e-6) * g

    cq_ref[...] = rms(h_ref[:, :Q_LORA], gq_ref[...]).astype(BF16)
    kc_ref[:, :KV_LORA] = rms(h_ref[:, Q_LORA:Q_LORA + KV_LORA], gkv_ref[...]).astype(BF16)
    c, s = c_ref[...], s_ref[...]
    o_idx = Q_LORA + KV_LORA
    ki = _layernorm(h_ref[:, o_idx:o_idx + IDX_DIM], gk_ref[...], bk_ref[...])
    ki_ref[...] = _rope128(ki, c, s, _swap_pair).astype(BF16)
    tail = h_ref[:, o_idx + IDX_DIM:]
    kc_ref[:, KV_LORA:] = _rope128(tail, c, s, _swap_pair)[:, :QK_ROPE].astype(BF16)
    wi_ref[...] = tail * (IDX_HEADS ** -0.5 * IDX_DIM ** -0.5)


def _l1_post(h1, g_q, g_kv, g_kidx, b_kidx, idx_tab, tm=256):
    T = h1.shape[0]
    row = lambda n: pl.BlockSpec((tm, n), lambda i: (i, 0))
    const = lambda n: pl.BlockSpec((1, n), lambda i: (0, 0))
    vec = lambda a: a.reshape(1, -1).astype(F32)
    return pl.pallas_call(
        _l1_post_kernel,
        grid=(T // tm,),
        in_specs=[row(L1_IN_PAD), const(Q_LORA), const(KV_LORA), const(IDX_DIM), const(IDX_DIM),
                  row(LANES), row(LANES)],
        out_specs=[row(Q_LORA), row(KV_LORA + QK_ROPE), row(IDX_DIM), row(LANES)],
        out_shape=[jax.ShapeDtypeStruct((T, Q_LORA), BF16), jax.ShapeDtypeStruct((T, KV_LORA + QK_ROPE), BF16),
                   jax.ShapeDtypeStruct((T, IDX_DIM), BF16), jax.ShapeDtypeStruct((T, LANES), F32)],
        compiler_params=_cparams(("parallel",)),
        name="l1_post",
    )(h1, vec(g_q), vec(g_kv), vec(g_kidx), vec(b_kidx), idx_tab[0], idx_tab[1])


def _qlat_kernel(qn_ref, wuk_ref, qpe_ref, o_ref):
    ql = _dot_nt(qn_ref[...], wuk_ref[...].astype(BF16))
    o_ref[:, :KV_LORA] = ql.astype(BF16)
    pe = qpe_ref[...]
    odd = pl.program_id(0) % 2 == 1

    @pl.when(jnp.logical_not(odd))
    def _():
        o_ref[:, KV_LORA:] = pe[:, :QK_ROPE]

    @pl.when(odd)
    def _():
        o_ref[:, KV_LORA:] = pe[:, QK_ROPE:]


def _q_absorb(q, w_uk, tm=1024):
    T = q.shape[0]
    tm = min(tm, T)
    H = MLA_HEADS
    pe0 = H * QK_NOPE // LANES
    return pl.pallas_call(
        _qlat_kernel,
        grid=(H, T // tm),
        in_specs=[pl.BlockSpec((tm, QK_NOPE), lambda h, i: (i, h)),
                  pl.BlockSpec((KV_LORA, QK_NOPE), lambda h, i: (0, h)),
                  pl.BlockSpec((tm, LANES), lambda h, i: (i, pe0 + h // 2))],
        out_specs=pl.BlockSpec((None, tm, KV_LORA + QK_ROPE), lambda h, i: (h, i, 0)),
        out_shape=jax.ShapeDtypeStruct((H, T, KV_LORA + QK_ROPE), BF16),
        compiler_params=_cparams(("parallel", "parallel")),
        name="q_absorb",
    )(q, w_uk, q)


DSA_TQ = 64
DSA_KB = 256
DSA_TK = 512
IDX_TQ = 128


def _indexer_kernel(qi_ref, w_ref, ki_ref, bias_ref, key_ref, *, T, tq, ts, n_sel):
    i = pl.program_id(0)
    kb = DSA_KB
    nkb = T // kb
    n_live = ((i + 1) * tq + kb - 1) // kb
    q = qi_ref[...].reshape(IDX_HEADS * tq, IDX_DIM)
    w = w_ref[...]

    def score_block(j, carry):
        rows = pl.ds(pl.multiple_of(j * kb, kb), kb)
        s = _dot_nt(q, ki_ref[rows, :])
        isc = jnp.sum(jnp.maximum(s, 0.0).reshape(IDX_HEADS, tq, kb) * w, axis=0) + 0.0
        qpos = i * tq + lax.broadcasted_iota(I32, (tq, kb), 0)
        kpos = j * kb + lax.broadcasted_iota(I32, (tq, kb), 1)
        isc = jnp.where(kpos <= qpos, isc, -jnp.inf)
        bits = pltpu.bitcast(isc, I32)
        key_ref[j] = jnp.where(bits < 0, bits ^ 0x7FFFFFFF, bits)
        return carry

    lax.fori_loop(0, n_live, score_block, 0)

    parts = [slice(r, r + ts) for r in range(0, tq, ts)]

    def count_ge(cand):
        cbs = [jnp.broadcast_to(cand[r], (ts, LANES)) for r in parts]

        def body(j, cs):
            out = []
            for r, cb, c in zip(parts, cbs, cs):
                blk = key_ref[j, r, :]
                for g in range(kb // LANES):
                    c = c + (blk[:, g * LANES:(g + 1) * LANES] >= cb).astype(F32)
                out.append(c)
            return tuple(out)

        cs = lax.fori_loop(0, n_live, body, tuple(jnp.zeros((ts, LANES), F32) for _ in parts))
        return jnp.concatenate([jnp.sum(c, axis=1, keepdims=True) for c in cs], axis=0)

    thr = jnp.full((tq, 1), -2**31, I32)
    for bit in range(31, -1, -1):
        cand = jnp.zeros((tq, 1), I32) if bit == 31 else thr + (1 << bit)
        thr = jnp.where(count_ge(cand) >= n_sel, cand, thr)

    def write_block(j, carry):
        blk = key_ref[j]
        mask = jnp.where((blk >= thr) & (blk > KEY_NEG_INF), 0.0, NEG).astype(BF16)
        for t, r in enumerate(parts):
            bias_ref[t, j] = mask[r]
        return carry

    lax.fori_loop(0, n_live, write_block, 0)

    def dead_block(j, carry):
        for t in range(len(parts)):
            bias_ref[t, j] = jnp.full((ts, kb), NEG, BF16)
        return carry

    lax.fori_loop(n_live, nkb, dead_block, 0)


def _indexer_mask(qi_h, w_col, ki, T, n_sel):
    tq, ts, kb = min(IDX_TQ, T), min(DSA_TQ, T), DSA_KB
    nkb = T // kb
    assert tq % ts == 0 and T % tq == 0
    return pl.pallas_call(
        functools.partial(_indexer_kernel, T=T, tq=tq, ts=ts, n_sel=n_sel),
        grid=(T // tq,),
        in_specs=[pl.BlockSpec((IDX_HEADS, tq, IDX_DIM), lambda i: (0, i, 0)),
                  pl.BlockSpec((IDX_HEADS, tq, 1), lambda i: (0, i, 0)),
                  pl.BlockSpec((T, IDX_DIM), lambda i: (0, 0), pipeline_mode=pl.Buffered(1))],
        out_specs=pl.BlockSpec((tq // ts, nkb, ts, kb), lambda i: (i, 0, 0, 0)),
        out_shape=jax.ShapeDtypeStruct((T // ts, nkb, ts, kb), BF16),
        scratch_shapes=[pltpu.VMEM((nkb, tq, kb), I32)],
        compiler_params=_cparams(("parallel",)),
        name="dsa_indexer",
    )(qi_h, w_col, ki)


def _dsa_kernel(q_ref, bias_ref, kc_ref, wuv_ref, o_ref, s_ref, p_ref, a_ref, m_ref, l_ref, acc_ref, *, tq, scale):
    i = pl.program_id(0)
    H, tk = MLA_HEADS, DSA_TK
    R = H * tq
    per = tk // DSA_KB
    n_steps = ((i + 1) * tq + tk - 1) // tk
    c = scale * LOG2E
    q = q_ref[...].reshape(R, KV_LORA + QK_ROPE)

    def keys(j):
        return kc_ref[pl.ds(pl.multiple_of(j * tk, tk), tk), :]

    m_ref[...] = jnp.full_like(m_ref, NEG)
    l_ref[...] = jnp.zeros_like(l_ref)
    acc_ref[...] = jnp.zeros_like(acc_ref)
    p_ref[1] = jnp.zeros((R, tk), BF16)
    a_ref[1] = jnp.ones((R, LANES), F32)
    s_ref[0] = _dot_nt(q, keys(0))
    lane_tiles = [slice(t * LANES, (t + 1) * LANES) for t in range(tk // LANES)]

    def add_values(slot, j):
        pv = _dot(p_ref[slot], keys(j)[:, :KV_LORA])
        a = a_ref[slot]
        for t in range(KV_LORA // LANES):
            cols = slice(t * LANES, (t + 1) * LANES)
            acc_ref[:, cols] = a * acc_ref[:, cols] + pv[:, cols]

    def step(j, cur):
        s_ref[1 - cur] = _dot_nt(q, keys(jnp.minimum(j + 1, n_steps - 1)))
        add_values(1 - cur, jnp.maximum(j - 1, 0))
        bias = jnp.concatenate([bias_ref[j * per + b] for b in range(per)], axis=1).astype(F32)
        for h in range(H):
            r = slice(h * tq, (h + 1) * tq)
            s = s_ref[cur, r, :] + bias
            m_prev = m_ref[r, :]
            m_new = jnp.maximum(m_prev, jnp.max(s, axis=1, keepdims=True))
            alpha = jnp.exp2((m_prev - m_new) * c)
            p = [jnp.exp2((s[:, t] - m_new) * c) for t in lane_tiles]
            l_ref[r, :] = alpha * l_ref[r, :] + sum(p[1:], p[0])
            m_ref[r, :] = m_new
            a_ref[cur, r, :] = alpha
            for t, pt in zip(lane_tiles, p):
                p_ref[cur, r, t] = pt.astype(BF16)

    def parity_step(j, carry):
        pl.when(j % 2 == 0)(lambda: step(j, 0))
        pl.when(j % 2 == 1)(lambda: step(j, 1))
        return carry

    lax.fori_loop(0, n_steps, parity_step, 0)
    add_values((n_steps - 1) % 2, n_steps - 1)

    for h in range(H):
        r = slice(h * tq, (h + 1) * tq)
        l = jnp.sum(l_ref[r, :], axis=1, keepdims=True)
        o_lat = (acc_ref[r, :] / l).astype(BF16)
        o_ref[:, h * V_HEAD:(h + 1) * V_HEAD] = _dot(o_lat, wuv_ref[h]).astype(o_ref.dtype)


def _dsa_attention(q576, bias, kc, w_uv_h, T):
    tq = min(DSA_TQ, T)
    H = MLA_HEADS
    C = KV_LORA + QK_ROPE
    nkb = T // DSA_KB
    assert T % DSA_TK == 0
    return pl.pallas_call(
        functools.partial(_dsa_kernel, tq=tq, scale=(QK_NOPE + QK_ROPE) ** -0.5),
        grid=(T // tq,),
        in_specs=[pl.BlockSpec((H, tq, C), lambda i: (0, i, 0)),
                  pl.BlockSpec((None, nkb, tq, DSA_KB), lambda i: (i, 0, 0, 0)),
                  pl.BlockSpec((T, C), lambda i: (0, 0), pipeline_mode=pl.Buffered(1)),
                  pl.BlockSpec((H, KV_LORA, V_HEAD), lambda i: (0, 0, 0), pipeline_mode=pl.Buffered(1))],
        out_specs=pl.BlockSpec((tq, H * V_HEAD), lambda i: (i, 0)),
        out_shape=jax.ShapeDtypeStruct((T, H * V_HEAD), BF16),
        scratch_shapes=[pltpu.VMEM((2, H * tq, DSA_TK), F32), pltpu.VMEM((2, H * tq, DSA_TK), BF16),
                        pltpu.VMEM((2, H * tq, LANES), F32),
                        pltpu.VMEM((H * tq, LANES), F32), pltpu.VMEM((H * tq, LANES), F32),
                        pltpu.VMEM((H * tq, KV_LORA), F32)],
        compiler_params=_cparams(("parallel",)),
        name="dsa_attention",
    )(q576, bias, kc, w_uv_h)


def _even_mixer(xb, T, w_in, w_out, lq1, lk1, lq2, lk2, g_subln, layer_idx, tabs):
    full = tabs[0]
    rope = [(0, 2 * MOBA_W, full, _swap_full),
            (3 * MOBA_W, 3 * MOBA_W + 2 * DIFF_W, full, _swap_full)]
    h0 = _matmul(xb, w_in, tm=2048, tn=1024, tk=1024, out_dtype=BF16, rope=rope, name="l0_in_proj")
    o_moba = _moba_attention(h0, T)
    o_diff = _diff_attention(h0, T, lq1, lk1, lq2, lk2, g_subln, layer_idx)
    o = jnp.concatenate([o_moba, o_diff], axis=1)
    return _matmul(o, w_out, tm=2048, tn=1024, tk=1024, out_dtype=F32, name="l0_out_proj")


def _odd_mixer(xb, T, w_in, g_q, g_kv, w_qb, w_uk, w_uv, w_iq, g_kidx, b_kidx, w_out, tabs):
    _, pair, idx = tabs
    D = w_in.shape[0]
    H = MLA_HEADS
    o_kpe, o_kidx, o_w = Q_LORA + KV_LORA, Q_LORA + KV_LORA + QK_ROPE, Q_LORA + KV_LORA + QK_ROPE + IDX_DIM
    w_in_p = jnp.concatenate([w_in[:, :o_kpe], w_in[:, o_kidx:o_w], w_in[:, o_kpe:o_kidx], w_in[:, o_w:],
                              jnp.zeros((D, L1_IN_PAD - w_in.shape[1]), w_in.dtype)], axis=1)
    h1 = _matmul(xb, w_in_p, tm=2048, tn=L1_IN_PAD // 2, tk=512, out_dtype=F32, name="l1_in_proj")
    cq, kc, ki, wi_full = _l1_post(h1, g_q, g_kv, g_kidx, b_kidx, idx)
    w_qb3 = w_qb.reshape(Q_LORA, H, QK_NOPE + QK_ROPE)
    w_qb_p = jnp.concatenate([w_qb3[:, :, :QK_NOPE].reshape(Q_LORA, H * QK_NOPE),
                              w_qb3[:, :, QK_NOPE:].reshape(Q_LORA, H * QK_ROPE)], axis=1)
    n0 = H * QK_NOPE
    q = _matmul(cq, w_qb_p, tm=2048, tn=1024, tk=Q_LORA, out_dtype=BF16,
                rope=[(n0, n0 + H * QK_ROPE, pair, _swap_pair)], name="l1_q_proj")
    qi_h = _matmul(cq, w_iq, tm=2048, tn=1024, tk=Q_LORA, out_dtype=BF16, head_major=True,
                   rope=[(0, IDX_HEADS * IDX_DIM, idx, _swap_pair)], name="l1_qi_proj")
    q576 = _q_absorb(q, w_uk)
    w_col = wi_full[:, QK_ROPE:QK_ROPE + IDX_HEADS].T[:, :, None]
    bias = _indexer_mask(qi_h, w_col, ki, T, min(IDX_TOPK_MAX, T // 4))
    w_uv_h = w_uv.reshape(KV_LORA, H, V_HEAD).transpose(1, 0, 2).astype(BF16)
    o = _dsa_attention(q576, bias, kc, w_uv_h, T)
    return _matmul(o, w_out, tm=2048, tn=1024, tk=1024, out_dtype=F32, name="l1_out_proj")


def kernel(x, l0_w_in, l0_w_out, l0_lam_q1, l0_lam_k1, l0_lam_q2, l0_lam_k2, l0_g_subln, l1_w_in, l1_g_q, l1_g_kv, l1_w_qb, l1_w_uk, l1_w_uv, l1_w_iq, l1_g_kidx, l1_b_kidx, l1_w_out, ln_g, ln_b, moe_w_router, moe_b_router, moe_w_gu, moe_b_gu, moe_w_down, moe_b_down):
    B, T, D = x.shape
    assert B == 1
    xn = x.reshape(T, D)
    xb = xn.astype(BF16)
    tabs = _rope_tables(T)
    for i in range(DEPTH):
        if i % 2 == 0:
            mix = _even_mixer(xb, T, l0_w_in, l0_w_out, l0_lam_q1, l0_lam_k1, l0_lam_q2, l0_lam_k2,
                              l0_g_subln, i, tabs)
        else:
            mix = _odd_mixer(xb, T, l1_w_in, l1_g_q, l1_g_kv, l1_w_qb, l1_w_uk, l1_w_uv, l1_w_iq,
                             l1_g_kidx, l1_b_kidx, l1_w_out, tabs)
        xn, xb, top_idx, gates = _mix_ln_route(xn, mix, ln_g[i, 0], ln_b[i, 0], moe_w_router[i], moe_b_router[i])
        xn, xb = _moe(xn, xb, top_idx, gates, moe_w_gu, moe_b_gu, moe_w_down, moe_b_down, i,
                      ln_g[i, 1], ln_b[i, 1])
    return xn.reshape(B, T, D)
```

```python
import functools
import math

import jax
import jax.numpy as jnp
from jax import lax
from jax.experimental import pallas as pl
from jax.experimental.pallas import tpu as pltpu

F32 = jnp.float32
BF16 = jnp.bfloat16
I32 = jnp.int32

HEAD_DIM = 128
ROPE_THETA = 10000.0
MOBA_HEADS = 16
MOBA_BLOCK = 256
MOBA_TOPK = 3
DIFF_HEADS = 8
MOBA_W = MOBA_HEADS * HEAD_DIM
DIFF_W = DIFF_HEADS * 2 * HEAD_DIM
MLA_HEADS = 32
Q_LORA = 1024
KV_LORA = 512
QK_NOPE = 128
QK_ROPE = 64
V_HEAD = 128
IDX_HEADS = 32
IDX_DIM = 128
IDX_TOPK_MAX = 256
N_EXPERTS = 32
TOP_K = 4
SWIGLU_LIMIT = 7.0
SWIGLU_ALPHA = 1.702
DEPTH = 2
DN_ALPHA = (2 * DEPTH) ** 0.25
LN_EPS = 1e-5

LANES = 128
VMEM_LIMIT_BYTES = 56 * 2**20

NEG = -1e30
LOG2E = 1.4426950408889634
KEY_NEG_INF = -2139095041


def _cparams(semantics, flags=None):
    return pltpu.CompilerParams(dimension_semantics=semantics, vmem_limit_bytes=VMEM_LIMIT_BYTES, flags=flags)


def _dot(a, b):
    return jnp.dot(a, b, preferred_element_type=F32)


def _dot_nt(a, b, precision=None):
    return lax.dot_general(a, b, (((1,), (1,)), ((), ())), preferred_element_type=F32, precision=precision)


def _rope_tables(T):
    pos = jnp.arange(T).astype(F32)[:, None]
    inv64 = ROPE_THETA ** (-jnp.arange(64, dtype=F32) / 64)
    a64 = pos * inv64[None, :]
    c64, s64 = jnp.cos(a64), jnp.sin(a64)
    inv32 = ROPE_THETA ** (-jnp.arange(32, dtype=F32) / 32)
    a32 = pos * inv32[None, :]
    c32, s32 = jnp.cos(a32), jnp.sin(a32)
    one, zero = jnp.ones_like(c32), jnp.zeros_like(c32)
    full = (jnp.concatenate([c64, c64], 1), jnp.concatenate([-s64, s64], 1))
    pair = (jnp.concatenate([c32, c32, c32, c32], 1), jnp.concatenate([-s32, s32, -s32, s32], 1))
    idx = (jnp.concatenate([c32, c32, one, one], 1), jnp.concatenate([-s32, s32, zero, zero], 1))
    return full, pair, idx


def _swap_full(z):
    return pltpu.roll(z, 64, axis=1)


def _swap_pair(z):
    lane = lax.broadcasted_iota(I32, z.shape, 1)
    return jnp.where((lane % 64) < 32, pltpu.roll(z, 96, axis=1), pltpu.roll(z, 32, axis=1))


def _rope128(z, c, s, swap):
    return z * c + swap(z) * s


def _mm_kernel(*refs, nk, tn, rope_ranges, swaps, head_major):
    n_tab = len(swaps)
    a_ref, b_ref = refs[0], refs[1]
    tab_refs = refs[2:2 + 2 * n_tab]
    o_ref, acc_ref = refs[2 + 2 * n_tab], refs[3 + 2 * n_tab]
    k = pl.program_id(2)

    @pl.when(k == 0)
    def _():
        acc_ref[...] = jnp.zeros_like(acc_ref)

    acc_ref[...] += _dot(a_ref[...].astype(BF16), b_ref[...].astype(BF16))

    def store(g, val):
        if head_major:
            o_ref[g] = val.astype(o_ref.dtype)
        else:
            o_ref[:, g * LANES:(g + 1) * LANES] = val.astype(o_ref.dtype)

    def store_plain():
        if head_major:
            for g in range(tn // LANES):
                store(g, acc_ref[:, g * LANES:(g + 1) * LANES])
        else:
            o_ref[...] = acc_ref[...].astype(o_ref.dtype)

    @pl.when(k == nk - 1)
    def _():
        if not rope_ranges:
            store_plain()
            return
        j = pl.program_id(1)
        plain = None
        for (lo, hi, t) in rope_ranges:
            hit = (j >= lo) & (j < hi)
            plain = hit if plain is None else (plain | hit)

            @pl.when(hit)
            def _(t=t):
                c = tab_refs[2 * t][...]
                s = tab_refs[2 * t + 1][...]
                for g in range(tn // LANES):
                    store(g, _rope128(acc_ref[:, g * LANES:(g + 1) * LANES], c, s, swaps[t]))

        pl.when(jnp.logical_not(plain))(store_plain)


def _matmul(a, b, *, tm, tn, tk, out_dtype, rope=None, head_major=False, name="mm"):
    M, K = a.shape
    _, N = b.shape
    tm, tn, tk = min(tm, M), min(tn, N), min(tk, K)
    assert M % tm == 0 and N % tn == 0 and K % tk == 0
    rope = rope or []
    ranges, tabs, swaps = [], [], []
    for t, (lo, hi, (c, s), swap) in enumerate(rope):
        assert lo % tn == 0 and hi % tn == 0 and tn % LANES == 0
        ranges.append((lo // tn, hi // tn, t))
        tabs += [c, s]
        swaps.append(swap)
    nk = K // tk
    in_specs = [pl.BlockSpec((tm, tk), lambda i, j, k: (i, k)),
                pl.BlockSpec((tk, tn), lambda i, j, k: (k, j))]
    in_specs += [pl.BlockSpec((tm, LANES), lambda i, j, k: (i, 0)) for _ in tabs]
    if head_major:
        assert tn % LANES == 0
        out_spec = pl.BlockSpec((tn // LANES, tm, LANES), lambda i, j, k: (j, i, 0))
        out_shape = jax.ShapeDtypeStruct((N // LANES, M, LANES), out_dtype)
    else:
        out_spec = pl.BlockSpec((tm, tn), lambda i, j, k: (i, j))
        out_shape = jax.ShapeDtypeStruct((M, N), out_dtype)
    return pl.pallas_call(
        functools.partial(_mm_kernel, nk=nk, tn=tn, rope_ranges=tuple(ranges), swaps=tuple(swaps),
                          head_major=head_major),
        grid=(M // tm, N // tn, nk),
        in_specs=in_specs,
        out_specs=out_spec,
        out_shape=out_shape,
        scratch_shapes=[pltpu.VMEM((tm, tn), F32)],
        compiler_params=_cparams(("parallel", "parallel", "arbitrary")),
        name=name,
    )(a, b, *tabs)


SOFTMAX_ROWS = 64


def _softmax_step(s, v, m_ref, l_ref, acc_ref, scale):
    c = scale * LOG2E
    tq, tk = s.shape
    dv = v.shape[1]
    rows = [slice(r, r + SOFTMAX_ROWS) for r in range(0, tq, SOFTMAX_ROWS)]
    p_rows, alphas = [], []
    for r in rows:
        sc = s[r, :]
        m_prev = m_ref[r, :]
        m_new = jnp.maximum(m_prev, jnp.max(sc, axis=1, keepdims=True))
        alpha = jnp.exp2((m_prev - m_new) * c)
        p = [jnp.exp2((sc[:, t:t + LANES] - m_new) * c) for t in range(0, tk, LANES)]
        l_ref[r, :] = alpha * l_ref[r, :] + sum(p[1:], p[0])
        m_ref[r, :] = m_new
        p_rows.append(jnp.concatenate([pt.astype(BF16) for pt in p], axis=1))
        alphas.append(alpha)
    pv = _dot(jnp.concatenate(p_rows, axis=0), v)
    for r, alpha in zip(rows, alphas):
        for t in range(0, dv, LANES):
            acc_ref[r, t:t + LANES] = alpha * acc_ref[r, t:t + LANES] + pv[r, t:t + LANES]


def _row_sum(l_ref):
    return jnp.sum(l_ref[...], axis=1, keepdims=True)


def _causal_mask(s, row0, col0):
    row = row0 + lax.broadcasted_iota(I32, s.shape, 0)
    col = col0 + lax.broadcasted_iota(I32, s.shape, 1)
    return jnp.where(col <= row, s, NEG)


MOBA_TILE = 1024


def _moba_kernel(q_ref, k_ref, v_ref, o_ref, kmean_ref, m_ref, l_ref, acc_ref, *, T, tq, scale):
    i = pl.program_id(1)
    nb = T // MOBA_BLOCK
    per = tq // MOBA_BLOCK

    @pl.when(i == 0)
    def _():
        blk = lax.broadcasted_iota(I32, (LANES, T), 0)
        pos = lax.broadcasted_iota(I32, (LANES, T), 1)
        avg = jnp.where(pos // MOBA_BLOCK == blk, 1.0 / MOBA_BLOCK, 0.0).astype(BF16)
        kmean_ref[...] = _dot(avg, k_ref[...])

    q = q_ref[...]
    gate = _dot_nt(q.astype(F32), kmean_ref[...], precision=lax.Precision.HIGHEST)
    lane = lax.broadcasted_iota(I32, (tq, LANES), 1)
    lane_f = lane.astype(F32)
    own = i * per + lax.broadcasted_iota(I32, (tq, LANES), 0) // MOBA_BLOCK
    past = lane < own
    g = jnp.where(past, gate, -jnp.inf)
    sel = jnp.zeros((tq, LANES), jnp.bool_)
    for _ in range(min(MOBA_TOPK, nb)):
        mx = jnp.max(g, axis=1, keepdims=True)
        first = jnp.min(jnp.where(g == mx, lane_f, float(LANES)), axis=1, keepdims=True)
        pick = lane_f == first
        sel = sel | pick
        g = jnp.where(pick, -jnp.inf, g)
    visible = (sel & past) | (lane == own)
    q_aug = jnp.concatenate([q, jnp.where(visible, 0.0, NEG).astype(BF16)], axis=1)

    m_ref[...] = jnp.full_like(m_ref, NEG)
    l_ref[...] = jnp.zeros_like(l_ref)
    acc_ref[...] = jnp.zeros_like(acc_ref)
    key_blk = lax.broadcasted_iota(I32, (tq, LANES), 0) // MOBA_BLOCK
    key_lane = lax.broadcasted_iota(I32, (tq, LANES), 1)

    def tile(j, causal):
        rows = pl.ds(pl.multiple_of(j * tq, tq), tq)
        k_aug = jnp.concatenate([k_ref[rows, :], (key_lane == j * per + key_blk).astype(BF16)], axis=1)
        s = _dot_nt(q_aug, k_aug)
        if causal:
            s = _causal_mask(s, 0, 0)
        _softmax_step(s, v_ref[rows, :], m_ref, l_ref, acc_ref, scale)

    def past_tile(j, carry):
        tile(j, False)
        return carry

    lax.fori_loop(0, i, past_tile, 0)
    tile(i, True)
    o_ref[...] = (acc_ref[...] / _row_sum(l_ref)).astype(o_ref.dtype)


def _moba_attention(h0, T):
    tq = min(MOBA_TILE, T)
    assert T % tq == 0 and tq % MOBA_BLOCK == 0 and T // MOBA_BLOCK <= LANES
    H = MOBA_HEADS
    return pl.pallas_call(
        functools.partial(_moba_kernel, T=T, tq=tq, scale=HEAD_DIM ** -0.5),
        grid=(H, T // tq),
        in_specs=[pl.BlockSpec((tq, HEAD_DIM), lambda h, i: (i, h)),
                  pl.BlockSpec((T, HEAD_DIM), lambda h, i: (0, H + h)),
                  pl.BlockSpec((T, HEAD_DIM), lambda h, i: (0, 2 * H + h))],
        out_specs=pl.BlockSpec((tq, HEAD_DIM), lambda h, i: (i, h)),
        out_shape=jax.ShapeDtypeStruct((T, MOBA_W), BF16),
        scratch_shapes=[pltpu.VMEM((LANES, HEAD_DIM), F32),
                        pltpu.VMEM((tq, LANES), F32), pltpu.VMEM((tq, LANES), F32),
                        pltpu.VMEM((tq, HEAD_DIM), F32)],
        compiler_params=_cparams(("parallel", "arbitrary")),
        name="moba_attention",
    )(h0, h0, h0)


DIFF_TQ = 1024


def _diff_kernel(lq1_ref, lk1_ref, lq2_ref, lk2_ref, g_ref, q_ref, k_ref, v_ref, o_ref,
                 m_ref, l_ref, acc_ref, *, tq, scale, lam_init):
    i = pl.program_id(1)
    m_ref[...] = jnp.full_like(m_ref, NEG)
    l_ref[...] = jnp.zeros_like(l_ref)
    acc_ref[...] = jnp.zeros_like(acc_ref)
    q = q_ref[...]

    def block(j, masked):
        rows = pl.ds(pl.multiple_of(j * tq, tq), tq)
        kj = k_ref[rows, :]
        vj = v_ref[rows, :]
        cols = [slice(mp * HEAD_DIM, (mp + 1) * HEAD_DIM) for mp in range(2)]
        scores = [_dot_nt(q[:, c], kj[:, c]) for c in cols]
        for mp in range(2):
            s = _causal_mask(scores[mp], 0, 0) if masked else scores[mp]
            _softmax_step(s, vj, m_ref.at[mp], l_ref.at[mp], acc_ref.at[mp], scale)

    def past_block(j, carry):
        block(j, False)
        return carry

    lax.fori_loop(0, i, past_block, 0)
    block(i, True)

    lam = (jnp.exp(jnp.sum(lq1_ref[...] * lk1_ref[...], axis=1, keepdims=True))
           - jnp.exp(jnp.sum(lq2_ref[...] * lk2_ref[...], axis=1, keepdims=True)) + lam_init)
    o = acc_ref[0] / _row_sum(l_ref.at[0]) - lam * (acc_ref[1] / _row_sum(l_ref.at[1]))
    o = o * lax.rsqrt(jnp.mean(o * o, axis=1, keepdims=True) + 1e-5) * g_ref[...]
    o_ref[...] = (o * (1.0 - lam_init)).astype(o_ref.dtype)


def _diff_attention(h0, T, lq1, lk1, lq2, lk2, g_subln, layer_idx):
    tq = min(DIFF_TQ, T)
    assert T % tq == 0
    W = 2 * HEAD_DIM
    q0, k0, v0 = 3 * MOBA_W // W, (3 * MOBA_W + DIFF_W) // W, (3 * MOBA_W + 2 * DIFF_W) // W
    lam_init = 0.8 - 0.6 * math.exp(-0.3 * layer_idx)
    vec = lambda a: a.reshape(1, -1).astype(F32)
    small = lambda n: pl.BlockSpec((1, n), lambda h, i: (0, 0))
    return pl.pallas_call(
        functools.partial(_diff_kernel, tq=tq, scale=HEAD_DIM ** -0.5, lam_init=lam_init),
        grid=(DIFF_HEADS, T // tq),
        in_specs=[small(HEAD_DIM)] * 4 + [small(W),
                  pl.BlockSpec((tq, W), lambda h, i: (i, q0 + h)),
                  pl.BlockSpec((T, W), lambda h, i: (0, k0 + h)),
                  pl.BlockSpec((T, W), lambda h, i: (0, v0 + h))],
        out_specs=pl.BlockSpec((tq, W), lambda h, i: (i, h)),
        out_shape=jax.ShapeDtypeStruct((T, DIFF_W), BF16),
        scratch_shapes=[pltpu.VMEM((2, tq, LANES), F32), pltpu.VMEM((2, tq, LANES), F32),
                        pltpu.VMEM((2, tq, W), F32)],
        compiler_params=_cparams(("parallel", "arbitrary")),
        name="diff_attention",
    )(vec(lq1), vec(lk1), vec(lq2), vec(lk2), vec(g_subln), h0, h0, h0)


def _layernorm(v, g, b):
    mu = jnp.mean(v, axis=1, keepdims=True)
    d = v - mu
    var = jnp.mean(d * d, axis=1, keepdims=True)
    return d * lax.rsqrt(var + LN_EPS) * g + b


def _route(xn, wr_ref, br_ref, idx_ref, gate_ref):
    logits = jnp.dot(xn, wr_ref[...], preferred_element_type=F32,
                     precision=lax.Precision.HIGHEST) + br_ref[...]
    lane = lax.broadcasted_iota(I32, logits.shape, 1)
    lane_f = lane.astype(F32)
    idx_out = jnp.zeros(logits.shape, F32)
    val_out = jnp.full(logits.shape, NEG, F32)
    g = logits
    for r in range(TOP_K):
        mx = jnp.max(g, axis=1, keepdims=True)
        first = jnp.min(jnp.where(g == mx, lane_f, float(LANES)), axis=1, keepdims=True)
        idx_out = jnp.where(lane == r, first, idx_out)
        val_out = jnp.where(lane == r, mx, val_out)
        g = jnp.where(lane_f == first, -jnp.inf, g)
    e = jnp.exp(val_out - jnp.max(val_out, axis=1, keepdims=True))
    gate_ref[...] = e / jnp.sum(e, axis=1, keepdims=True)
    idx_ref[...] = idx_out.astype(I32)


def _mix_ln_kernel(x_ref, mix_ref, g_ref, b_ref, wr_ref, br_ref, xn_ref, xb_ref, idx_ref, gate_ref):
    xn = _layernorm(DN_ALPHA * x_ref[...] + mix_ref[...], g_ref[...], b_ref[...])
    xn_ref[...] = xn
    xb_ref[...] = xn.astype(BF16)
    _route(xn, wr_ref, br_ref, idx_ref, gate_ref)


def _mix_ln_route(x, mix, g, b, w_router, b_router, tm=256):
    T, D = x.shape
    E = w_router.shape[1]
    wr = jnp.pad(w_router.astype(F32), ((0, 0), (0, LANES - E)))
    br = jnp.pad(b_router.astype(F32).reshape(1, E), ((0, 0), (0, LANES - E)), constant_values=NEG)
    row = lambda n: pl.BlockSpec((tm, n), lambda i: (i, 0))
    const = lambda r, n: pl.BlockSpec((r, n), lambda i: (0, 0))
    return pl.pallas_call(
        _mix_ln_kernel,
        grid=(T // tm,),
        in_specs=[row(D), row(D), const(1, D), const(1, D), const(D, LANES), const(1, LANES)],
        out_specs=[row(D), row(D), row(LANES), row(LANES)],
        out_shape=[jax.ShapeDtypeStruct((T, D), F32), jax.ShapeDtypeStruct((T, D), BF16),
                   jax.ShapeDtypeStruct((T, LANES), I32), jax.ShapeDtypeStruct((T, LANES), F32)],
        compiler_params=_cparams(("parallel",)),
        name="mix_ln_route",
    )(x, mix, g.reshape(1, D).astype(F32), b.reshape(1, D).astype(F32), wr, br)


def _combine_ln_kernel(x_ref, y_ref, gate_ref, g_ref, b_ref, xn_ref, xb_ref):
    gates = gate_ref[...]
    f = jnp.zeros(x_ref.shape, F32)
    for r in range(TOP_K):
        f = f + gates[:, r:r + 1] * y_ref[r].astype(F32)
    xn = _layernorm(DN_ALPHA * x_ref[...] + f, g_ref[...], b_ref[...])
    xn_ref[...] = xn
    xb_ref[...] = xn.astype(BF16)


def _combine_ln(x, y4, gates, g, b, tm=256):
    T, D = x.shape
    row = lambda n: pl.BlockSpec((tm, n), lambda i: (i, 0))
    const = lambda r, n: pl.BlockSpec((r, n), lambda i: (0, 0))
    return pl.pallas_call(
        _combine_ln_kernel,
        grid=(T // tm,),
        in_specs=[row(D), pl.BlockSpec((TOP_K, tm, D), lambda i: (0, i, 0)), row(LANES), const(1, D), const(1, D)],
        out_specs=[row(D), row(D)],
        out_shape=[jax.ShapeDtypeStruct((T, D), F32), jax.ShapeDtypeStruct((T, D), BF16)],
        compiler_params=_cparams(("parallel",)),
        name="moe_combine_ln",
    )(x, y4, gates, g.reshape(1, D).astype(F32), b.reshape(1, D).astype(F32))


MOE_TM = 1152
MOE_FC = 256
MOE_DC = 1024
MOE_SPLIT = 16


def _expert_kernel(te_ref, na_ref, x_ref, wgu_ref, bgu_ref, wd_ref, bd_ref, sel_ref, *rest, ng, fc, base):
    y_ref, act_ref = rest[-2:]
    s = pl.program_id(1)
    live = base + pl.program_id(0) < na_ref[0]

    @pl.when(live & (s < ng))
    def _():
        hg = _dot(x_ref[...], wgu_ref[...].astype(BF16)) + bgu_ref[...]
        gate = jnp.minimum(hg, SWIGLU_LIMIT)
        up = jnp.clip(hg, -SWIGLU_LIMIT, SWIGLU_LIMIT)
        up_next = pltpu.roll(up, 2 * fc - 1, axis=1)
        act = (up_next + 1.0) * (gate * (1.0 / (1.0 + jnp.exp(-SWIGLU_ALPHA * gate))))
        act_ref[s] = _dot(act.astype(BF16), sel_ref[...]).astype(BF16)

    @pl.when(live & (s >= ng))
    def _():
        y = bd_ref[...] + _dot(act_ref[0], wd_ref[0:fc, :].astype(BF16))
        for c in range(1, ng):
            y = y + _dot(act_ref[c], wd_ref[c * fc:(c + 1) * fc, :].astype(BF16))
        y_ref[...] = y.astype(y_ref.dtype)

    @pl.when(jnp.logical_not(live) & (s >= ng))
    def _():
        y_ref[...] = jnp.zeros_like(y_ref)


def _experts(xs, tile_expert, n_active, w_gu, b_gu, w_down, b_down, layer, base, n_tiles, y_prev=None):
    rows, D = xs.shape
    _, E, _, F2 = w_gu.shape
    F = F2 // 2
    tm, fc, dc = MOE_TM, min(MOE_FC, F), min(MOE_DC, D)
    ng, nd = F // fc, D // dc
    nt = rows // tm
    even = (lax.broadcasted_iota(I32, (2 * fc, fc), 0) == 2 * lax.broadcasted_iota(I32, (2 * fc, fc), 1)).astype(BF16)

    def live(i, na):
        return jnp.clip(jnp.minimum(base + i, na[0] - 1), base, base + nt - 1)

    def g_chunk(i, s, na):
        return jnp.where(base + i < na[0], jnp.minimum(s, ng - 1), ng - 1)

    def d_chunk(i, s, na):
        return jnp.where(base + i < na[0], jnp.clip(s - ng, 0, nd - 1), nd - 1)

    in_specs = [
        pl.BlockSpec((tm, D), lambda i, s, te, na: (live(i, na) - base, 0), pipeline_mode=pl.Buffered(1)),
        pl.BlockSpec((None, None, D, 2 * fc), lambda i, s, te, na: (layer, te[live(i, na)], 0, g_chunk(i, s, na))),
        pl.BlockSpec((None, None, 1, 2 * fc), lambda i, s, te, na: (layer, te[live(i, na)], 0, g_chunk(i, s, na))),
        pl.BlockSpec((None, None, F, dc), lambda i, s, te, na: (layer, te[live(i, na)], 0, d_chunk(i, s, na))),
        pl.BlockSpec((None, None, 1, dc), lambda i, s, te, na: (layer, te[live(i, na)], 0, d_chunk(i, s, na))),
        pl.BlockSpec((2 * fc, fc), lambda i, s, te, na: (0, 0)),
    ]
    args = [tile_expert, n_active, xs, w_gu, b_gu.reshape(-1, E, 1, F2), w_down, b_down.reshape(-1, E, 1, D), even]
    aliases = {}
    if y_prev is not None:
        in_specs.append(pl.BlockSpec(memory_space=pl.ANY))
        aliases = {len(args): 0}
        args.append(y_prev)
    grid_spec = pltpu.PrefetchScalarGridSpec(
        num_scalar_prefetch=2,
        grid=(nt, ng + nd),
        in_specs=in_specs,
        out_specs=pl.BlockSpec((tm, dc), lambda i, s, te, na: (base + i, jnp.clip(s - ng, 0, nd - 1))),
        scratch_shapes=[pltpu.VMEM((ng, tm, fc), BF16)],
    )
    return pl.pallas_call(
        functools.partial(_expert_kernel, ng=ng, fc=fc, base=base),
        grid_spec=grid_spec,
        out_shape=jax.ShapeDtypeStruct((n_tiles * tm, D), BF16),
        input_output_aliases=aliases,
        compiler_params=_cparams(("arbitrary", "arbitrary")),
        name="moe_experts",
    )(*args)


def _moe(xn, xb, top_idx, gates, w_gu, b_gu, w_down, b_down, layer, ln_g, ln_b):
    T, D = xn.shape
    E = w_gu.shape[1]
    tm = MOE_TM
    e_flat = top_idx[:, :TOP_K].reshape(-1)
    onehot = (e_flat[:, None] == jnp.arange(E, dtype=I32)[None, :]).astype(I32)
    rank = jnp.take_along_axis(jnp.cumsum(onehot, axis=0), e_flat[:, None], axis=1)[:, 0] - 1
    counts = jnp.sum(onehot, axis=0)
    padded = ((counts + tm - 1) // tm) * tm
    ends = jnp.cumsum(padded)
    starts = ends - padded
    slot = starts[e_flat] + rank
    n_tiles = (T * TOP_K + E * (tm - 1)) // tm
    P = n_tiles * tm
    token_of_slot = (jnp.arange(P, dtype=I32) % T).at[slot].set(jnp.arange(T * TOP_K, dtype=I32) // TOP_K)
    tile_expert = jnp.minimum(jnp.searchsorted(ends, jnp.arange(n_tiles, dtype=I32) * tm, side="right"),
                              E - 1).astype(I32)
    n_active = (ends[-1:] // tm).astype(I32)

    split = min(MOE_SPLIT, n_tiles - 1)
    y = None
    for lo, hi in ((0, split), (split, n_tiles)):
        xs = jnp.take(xb, token_of_slot[lo * tm:hi * tm], axis=0, mode="clip")
        y = _experts(xs, tile_expert, n_active, w_gu, b_gu, w_down, b_down, layer, lo, n_tiles, y)
    slot_km = slot.reshape(T, TOP_K).T.reshape(-1)
    y4 = jnp.take(y, slot_km, axis=0, mode="clip").reshape(TOP_K, T, D)
    return _combine_ln(xn, y4, gates, ln_g, ln_b)


L1_IN_PAD = 1792


def _l1_post_kernel(h_ref, gq_ref, gkv_ref, gk_ref, bk_ref, c_ref, s_ref, cq_ref, kc_ref, ki_ref, wi_ref):
    def rms(v, g):
        return v * lax.rsqrt(jnp.mean(v * v, axis=1, keepdims=True) + 1e-6) * g

    cq_ref[...] = rms(h_ref[:, :Q_LORA], gq_ref[...]).astype(BF16)
    kc_ref[:, :KV_LORA] = rms(h_ref[:, Q_LORA:Q_LORA + KV_LORA], gkv_ref[...]).astype(BF16)
    c, s = c_ref[...], s_ref[...]
    o_idx = Q_LORA + KV_LORA
    ki = _layernorm(h_ref[:, o_idx:o_idx + IDX_DIM], gk_ref[...], bk_ref[...])
    ki_ref[...] = _rope128(ki, c, s, _swap_pair).astype(BF16)
    tail = h_ref[:, o_idx + IDX_DIM:]
    kc_ref[:, KV_LORA:] = _rope128(tail, c, s, _swap_pair)[:, :QK_ROPE].astype(BF16)
    wi_ref[...] = tail * (IDX_HEADS ** -0.5 * IDX_DIM ** -0.5)


def _l1_post(h1, g_q, g_kv, g_kidx, b_kidx, idx_tab, tm=256):
    T = h1.shape[0]
    row = lambda n: pl.BlockSpec((tm, n), lambda i: (i, 0))
    const = lambda n: pl.BlockSpec((1, n), lambda i: (0, 0))
    vec = lambda a: a.reshape(1, -1).astype(F32)
    return pl.pallas_call(
        _l1_post_kernel,
        grid=(T // tm,),
        in_specs=[row(L1_IN_PAD), const(Q_LORA), const(KV_LORA), const(IDX_DIM), const(IDX_DIM),
                  row(LANES), row(LANES)],
        out_specs=[row(Q_LORA), row(KV_LORA + QK_ROPE), row(IDX_DIM), row(LANES)],
        out_shape=[jax.ShapeDtypeStruct((T, Q_LORA), BF16), jax.ShapeDtypeStruct((T, KV_LORA + QK_ROPE), BF16),
                   jax.ShapeDtypeStruct((T, IDX_DIM), BF16), jax.ShapeDtypeStruct((T, LANES), F32)],
        compiler_params=_cparams(("parallel",)),
        name="l1_post",
    )(h1, vec(g_q), vec(g_kv), vec(g_kidx), vec(b_kidx), idx_tab[0], idx_tab[1])


def _qlat_kernel(qn_ref, wuk_ref, qpe_ref, o_ref):
    ql = _dot_nt(qn_ref[...], wuk_ref[...].astype(BF16))
    o_ref[:, :KV_LORA] = ql.astype(BF16)
    pe = qpe_ref[...]
    odd = pl.program_id(0) % 2 == 1

    @pl.when(jnp.logical_not(odd))
    def _():
        o_ref[:, KV_LORA:] = pe[:, :QK_ROPE]

    @pl.when(odd)
    def _():
        o_ref[:, KV_LORA:] = pe[:, QK_ROPE:]


def _q_absorb(q, w_uk, tm=1024):
    T = q.shape[0]
    tm = min(tm, T)
    H = MLA_HEADS
    pe0 = H * QK_NOPE // LANES
    return pl.pallas_call(
        _qlat_kernel,
        grid=(H, T // tm),
        in_specs=[pl.BlockSpec((tm, QK_NOPE), lambda h, i: (i, h)),
                  pl.BlockSpec((KV_LORA, QK_NOPE), lambda h, i: (0, h)),
                  pl.BlockSpec((tm, LANES), lambda h, i: (i, pe0 + h // 2))],
        out_specs=pl.BlockSpec((None, tm, KV_LORA + QK_ROPE), lambda h, i: (h, i, 0)),
        out_shape=jax.ShapeDtypeStruct((H, T, KV_LORA + QK_ROPE), BF16),
        compiler_params=_cparams(("parallel", "parallel")),
        name="q_absorb",
    )(q, w_uk, q)


DSA_TQ = 64
DSA_KB = 256
DSA_TK = 512
IDX_TQ = 256


def _indexer_kernel(qi_ref, w_ref, ki_ref, bias_ref, key_ref, *, T, tq, ts, n_sel):
    i = pl.program_id(0)
    kb = DSA_KB
    nkb = T // kb
    n_live = ((i + 1) * tq + kb - 1) // kb
    q = qi_ref[...].reshape(IDX_HEADS * tq, IDX_DIM)
    w = w_ref[...]

    def score_block(j, carry):
        rows = pl.ds(pl.multiple_of(j * kb, kb), kb)
        s = _dot_nt(q, ki_ref[rows, :])
        isc = jnp.sum(jnp.maximum(s, 0.0).reshape(IDX_HEADS, tq, kb) * w, axis=0) + 0.0
        qpos = i * tq + lax.broadcasted_iota(I32, (tq, kb), 0)
        kpos = j * kb + lax.broadcasted_iota(I32, (tq, kb), 1)
        isc = jnp.where(kpos <= qpos, isc, -jnp.inf)
        bits = pltpu.bitcast(isc, I32)
        key_ref[j] = jnp.where(bits < 0, bits ^ 0x7FFFFFFF, bits)
        return carry

    lax.fori_loop(0, n_live, score_block, 0)

    parts = [slice(r, r + ts) for r in range(0, tq, ts)]

    def count_ge(cand):
        cbs = [jnp.broadcast_to(cand[r], (ts, LANES)) for r in parts]

        def body(j, cs):
            out = []
            for r, cb, c in zip(parts, cbs, cs):
                blk = key_ref[j, r, :]
                for g in range(kb // LANES):
                    c = c + (blk[:, g * LANES:(g + 1) * LANES] >= cb).astype(F32)
                out.append(c)
            return tuple(out)

        cs = lax.fori_loop(0, n_live, body, tuple(jnp.zeros((ts, LANES), F32) for _ in parts))
        return jnp.concatenate([jnp.sum(c, axis=1, keepdims=True) for c in cs], axis=0)

    thr = jnp.full((tq, 1), -2**31, I32)
    for bit in range(31, -1, -1):
        cand = jnp.zeros((tq, 1), I32) if bit == 31 else thr + (1 << bit)
        thr = jnp.where(count_ge(cand) >= n_sel, cand, thr)

    def write_block(j, carry):
        blk = key_ref[j]
        mask = jnp.where((blk >= thr) & (blk > KEY_NEG_INF), 0.0, NEG).astype(BF16)
        for t, r in enumerate(parts):
            bias_ref[t, j] = mask[r]
        return carry

    lax.fori_loop(0, n_live, write_block, 0)

    def dead_block(j, carry):
        for t in range(len(parts)):
            bias_ref[t, j] = jnp.full((ts, kb), NEG, BF16)
        return carry

    lax.fori_loop(n_live, nkb, dead_block, 0)


def _indexer_mask(qi_h, w_col, ki, T, n_sel):
    tq, ts, kb = min(IDX_TQ, T), min(DSA_TQ, T), DSA_KB
    nkb = T // kb
    assert tq % ts == 0 and T % tq == 0
    return pl.pallas_call(
        functools.partial(_indexer_kernel, T=T, tq=tq, ts=ts, n_sel=n_sel),
        grid=(T // tq,),
        in_specs=[pl.BlockSpec((IDX_HEADS, tq, IDX_DIM), lambda i: (0, i, 0)),
                  pl.BlockSpec((IDX_HEADS, tq, 1), lambda i: (0, i, 0)),
                  pl.BlockSpec((T, IDX_DIM), lambda i: (0, 0), pipeline_mode=pl.Buffered(1))],
        out_specs=pl.BlockSpec((tq // ts, nkb, ts, kb), lambda i: (i, 0, 0, 0)),
        out_shape=jax.ShapeDtypeStruct((T // ts, nkb, ts, kb), BF16),
        scratch_shapes=[pltpu.VMEM((nkb, tq, kb), I32)],
        compiler_params=_cparams(("parallel",)),
        name="dsa_indexer",
    )(qi_h, w_col, ki)


def _dsa_kernel(q_ref, bias_ref, kc_ref, wuv_ref, o_ref, s_ref, p_ref, a_ref, m_ref, l_ref, acc_ref, *, tq, scale):
    i = pl.program_id(0)
    H, tk = MLA_HEADS, DSA_TK
    R = H * tq
    per = tk // DSA_KB
    n_steps = ((i + 1) * tq + tk - 1) // tk
    c = scale * LOG2E
    q = q_ref[...].reshape(R, KV_LORA + QK_ROPE)

    def keys(j):
        return kc_ref[pl.ds(pl.multiple_of(j * tk, tk), tk), :]

    m_ref[...] = jnp.full_like(m_ref, NEG)
    l_ref[...] = jnp.zeros_like(l_ref)
    acc_ref[...] = jnp.zeros_like(acc_ref)
    p_ref[1] = jnp.zeros((R, tk), BF16)
    a_ref[1] = jnp.ones((R, LANES), F32)
    s_ref[0] = _dot_nt(q, keys(0))
    lane_tiles = [slice(t * LANES, (t + 1) * LANES) for t in range(tk // LANES)]

    def add_values(slot, j):
        pv = _dot(p_ref[slot], keys(j)[:, :KV_LORA])
        a = a_ref[slot]
        for t in range(KV_LORA // LANES):
            cols = slice(t * LANES, (t + 1) * LANES)
            acc_ref[:, cols] = a * acc_ref[:, cols] + pv[:, cols]

    def step(j, cur):
        s_ref[1 - cur] = _dot_nt(q, keys(jnp.minimum(j + 1, n_steps - 1)))
        add_values(1 - cur, jnp.maximum(j - 1, 0))
        bias = jnp.concatenate([bias_ref[j * per + b] for b in range(per)], axis=1).astype(F32)
        for h in range(H):
            r = slice(h * tq, (h + 1) * tq)
            s = s_ref[cur, r, :] + bias
            m_prev = m_ref[r, :]
            m_new = jnp.maximum(m_prev, jnp.max(s, axis=1, keepdims=True))
            alpha = jnp.exp2((m_prev - m_new) * c)
            p = [jnp.exp2((s[:, t] - m_new) * c) for t in lane_tiles]
            l_ref[r, :] = alpha * l_ref[r, :] + sum(p[1:], p[0])
            m_ref[r, :] = m_new
            a_ref[cur, r, :] = alpha
            for t, pt in zip(lane_tiles, p):
                p_ref[cur, r, t] = pt.astype(BF16)

    def parity_step(j, carry):
        pl.when(j % 2 == 0)(lambda: step(j, 0))
        pl.when(j % 2 == 1)(lambda: step(j, 1))
        return carry

    lax.fori_loop(0, n_steps, parity_step, 0)
    add_values((n_steps - 1) % 2, n_steps - 1)

    for h in range(H):
        r = slice(h * tq, (h + 1) * tq)
        l = jnp.sum(l_ref[r, :], axis=1, keepdims=True)
        o_lat = (acc_ref[r, :] / l).astype(BF16)
        o_ref[:, h * V_HEAD:(h + 1) * V_HEAD] = _dot(o_lat, wuv_ref[h]).astype(o_ref.dtype)


def _dsa_attention(q576, bias, kc, w_uv_h, T):
    tq = min(DSA_TQ, T)
    H = MLA_HEADS
    C = KV_LORA + QK_ROPE
    nkb = T // DSA_KB
    assert T % DSA_TK == 0
    return pl.pallas_call(
        functools.partial(_dsa_kernel, tq=tq, scale=(QK_NOPE + QK_ROPE) ** -0.5),
        grid=(T // tq,),
        in_specs=[pl.BlockSpec((H, tq, C), lambda i: (0, i, 0)),
                  pl.BlockSpec((None, nkb, tq, DSA_KB), lambda i: (i, 0, 0, 0)),
                  pl.BlockSpec((T, C), lambda i: (0, 0), pipeline_mode=pl.Buffered(1)),
                  pl.BlockSpec((H, KV_LORA, V_HEAD), lambda i: (0, 0, 0), pipeline_mode=pl.Buffered(1))],
        out_specs=pl.BlockSpec((tq, H * V_HEAD), lambda i: (i, 0)),
        out_shape=jax.ShapeDtypeStruct((T, H * V_HEAD), BF16),
        scratch_shapes=[pltpu.VMEM((2, H * tq, DSA_TK), F32), pltpu.VMEM((2, H * tq, DSA_TK), BF16),
                        pltpu.VMEM((2, H * tq, LANES), F32),
                        pltpu.VMEM((H * tq, LANES), F32), pltpu.VMEM((H * tq, LANES), F32),
                        pltpu.VMEM((H * tq, KV_LORA), F32)],
        compiler_params=_cparams(("parallel",)),
        name="dsa_attention",
    )(q576, bias, kc, w_uv_h)


def _even_mixer(xb, T, w_in, w_out, lq1, lk1, lq2, lk2, g_subln, layer_idx, tabs):
    full = tabs[0]
    rope = [(0, 2 * MOBA_W, full, _swap_full),
            (3 * MOBA_W, 3 * MOBA_W + 2 * DIFF_W, full, _swap_full)]
    h0 = _matmul(xb, w_in, tm=2048, tn=1024, tk=1024, out_dtype=BF16, rope=rope, name="l0_in_proj")
    o_moba = _moba_attention(h0, T)
    o_diff = _diff_attention(h0, T, lq1, lk1, lq2, lk2, g_subln, layer_idx)
    o = jnp.concatenate([o_moba, o_diff], axis=1)
    return _matmul(o, w_out, tm=2048, tn=1024, tk=1024, out_dtype=F32, name="l0_out_proj")


def _odd_mixer(xb, T, w_in, g_q, g_kv, w_qb, w_uk, w_uv, w_iq, g_kidx, b_kidx, w_out, tabs):
    _, pair, idx = tabs
    D = w_in.shape[0]
    H = MLA_HEADS
    o_kpe, o_kidx, o_w = Q_LORA + KV_LORA, Q_LORA + KV_LORA + QK_ROPE, Q_LORA + KV_LORA + QK_ROPE + IDX_DIM
    w_in_p = jnp.concatenate([w_in[:, :o_kpe], w_in[:, o_kidx:o_w], w_in[:, o_kpe:o_kidx], w_in[:, o_w:],
                              jnp.zeros((D, L1_IN_PAD - w_in.shape[1]), w_in.dtype)], axis=1)
    h1 = _matmul(xb, w_in_p, tm=2048, tn=L1_IN_PAD // 2, tk=512, out_dtype=F32, name="l1_in_proj")
    cq, kc, ki, wi_full = _l1_post(h1, g_q, g_kv, g_kidx, b_kidx, idx)
    w_qb3 = w_qb.reshape(Q_LORA, H, QK_NOPE + QK_ROPE)
    w_qb_p = jnp.concatenate([w_qb3[:, :, :QK_NOPE].reshape(Q_LORA, H * QK_NOPE),
                              w_qb3[:, :, QK_NOPE:].reshape(Q_LORA, H * QK_ROPE)], axis=1)
    n0 = H * QK_NOPE
    q = _matmul(cq, w_qb_p, tm=2048, tn=1024, tk=Q_LORA, out_dtype=BF16,
                rope=[(n0, n0 + H * QK_ROPE, pair, _swap_pair)], name="l1_q_proj")
    qi_h = _matmul(cq, w_iq, tm=2048, tn=1024, tk=Q_LORA, out_dtype=BF16, head_major=True,
                   rope=[(0, IDX_HEADS * IDX_DIM, idx, _swap_pair)], name="l1_qi_proj")
    q576 = _q_absorb(q, w_uk)
    w_col = wi_full[:, QK_ROPE:QK_ROPE + IDX_HEADS].T[:, :, None]
    bias = _indexer_mask(qi_h, w_col, ki, T, min(IDX_TOPK_MAX, T // 4))
    w_uv_h = w_uv.reshape(KV_LORA, H, V_HEAD).transpose(1, 0, 2).astype(BF16)
    o = _dsa_attention(q576, bias, kc, w_uv_h, T)
    return _matmul(o, w_out, tm=2048, tn=1024, tk=1024, out_dtype=F32, name="l1_out_proj")


def kernel(x, l0_w_in, l0_w_out, l0_lam_q1, l0_lam_k1, l0_lam_q2, l0_lam_k2, l0_g_subln, l1_w_in, l1_g_q, l1_g_kv, l1_w_qb, l1_w_uk, l1_w_uv, l1_w_iq, l1_g_kidx, l1_b_kidx, l1_w_out, ln_g, ln_b, moe_w_router, moe_b_router, moe_w_gu, moe_b_gu, moe_w_down, moe_b_down):
    B, T, D = x.shape
    assert B == 1
    xn = x.reshape(T, D)
    xb = xn.astype(BF16)
    tabs = _rope_tables(T)
    for i in range(DEPTH):
        if i % 2 == 0:
            mix = _even_mixer(xb, T, l0_w_in, l0_w_out, l0_lam_q1, l0_lam_k1, l0_lam_q2, l0_lam_k2,
                              l0_g_subln, i, tabs)
        else:
            mix = _odd_mixer(xb, T, l1_w_in, l1_g_q, l1_g_kv, l1_w_qb, l1_w_uk, l1_w_uv, l1_w_iq,
                             l1_g_kidx, l1_b_kidx, l1_w_out, tabs)
        xn, xb, top_idx, gates = _mix_ln_route(xn, mix, ln_g[i, 0], ln_b[i, 0], moe_w_router[i], moe_b_router[i])
        xn, xb = _moe(xn, xb, top_idx, gates, moe_w_gu, moe_b_gu, moe_w_down, moe_b_down, i,
                      ln_g[i, 1], ln_b[i, 1])
    return xn.reshape(B, T, D)
```
